```python
import jax, jax.numpy as jnp
from jax import lax
import numpy as np

D_MODEL = 1024
BATCH = 16
SEQ = 2048
DEPTH = 4

GRID_W = 64
CTX_LEN = 256
HEAD_DIM = 64
ROPE_THETA = 10000.0
EPS = 1e-6
ATTN_SCALE = HEAD_DIM ** -0.5
Q_BLOCK = 128

ATT_WIDTH = D_MODEL // 2
ATT_HEADS = ATT_WIDTH // HEAD_DIM
ATT_KV_HEADS = ATT_HEADS // 4
ATT_GROUP = ATT_HEADS // ATT_KV_HEADS
ATT_KV_WIDTH = ATT_KV_HEADS * HEAD_DIM
POOL_WINDOWS = (2, 4, 8, 16)
POOL_WIDTH = D_MODEL // 4
POOL_GROUP_DIM = POOL_WIDTH // len(POOL_WINDOWS)
NA_WIDTH = D_MODEL // 4
NA_HEADS = NA_WIDTH // HEAD_DIM
NA_KH_MAX = 8
NA_KW = 16

MIX_WIDTH = ATT_WIDTH + POOL_WIDTH + NA_WIDTH
IN_SPLITS = (ATT_WIDTH, ATT_KV_WIDTH, ATT_KV_WIDTH, ATT_WIDTH,
             POOL_WIDTH, POOL_WIDTH,
             NA_WIDTH, NA_WIDTH, NA_WIDTH, NA_WIDTH)
IN_OFFSETS = tuple(int(o) for o in np.cumsum((0,) + IN_SPLITS))
IN_WIDTH = IN_OFFSETS[-1]
SPLIT_POINTS = IN_OFFSETS[1:-1]

kernel_name = "hybrid_parallel_heads_diffusion_block"


def rms_norm(x, g):
    xf = x.astype(jnp.float32)
    y = xf * lax.rsqrt(jnp.mean(xf * xf, axis=-1, keepdims=True) + EPS)
    return (y * g.astype(jnp.float32)).astype(x.dtype)


def rope_axis(x, pos):
    half = x.shape[-1] // 2
    inv_freq = ROPE_THETA ** (-jnp.arange(half, dtype=jnp.float32) / half)
    ang = pos.astype(jnp.float32)[:, None] * inv_freq[None, :]
    cos = jnp.cos(ang)[None, :, None, :]
    sin = jnp.sin(ang)[None, :, None, :]
    xf = x.astype(jnp.float32)
    x1, x2 = xf[..., :half], xf[..., half:]
    return jnp.concatenate([x1 * cos - x2 * sin, x2 * cos + x1 * sin], axis=-1).astype(x.dtype)


def rope_2d(x, row, col):
    h = HEAD_DIM // 2
    return jnp.concatenate([rope_axis(x[..., :h], row), rope_axis(x[..., h:], col)], axis=-1)


def dense_attn(q, k, v):
    s = jnp.einsum("bqkgd,bskd->bkgqs", q, k).astype(jnp.float32)
    p = jax.nn.softmax(s, axis=-1).astype(v.dtype)
    o = jnp.einsum("bkgqs,bskd->bqkgd", p, v)
    return o.reshape(o.shape[0], o.shape[1], -1)


def global_gqa(q, k, v, k_ctx, v_ctx):
    bsz, seq = q.shape[:2]
    n_blk = seq // Q_BLOCK
    kk = jnp.concatenate([k, k_ctx], axis=1)
    vv = jnp.concatenate([v, v_ctx], axis=1)
    qb = jnp.moveaxis(q.reshape(bsz, n_blk, Q_BLOCK, ATT_KV_HEADS, ATT_GROUP, HEAD_DIM), 1, 0)
    o = lax.map(lambda qi: dense_attn(qi, kk, vv), qb)
    return jnp.moveaxis(o, 0, 1).reshape(bsz, seq, ATT_WIDTH)


def multi_pool(z):
    bsz, length, width = z.shape
    zf = z.astype(jnp.float32)
    cs = jnp.concatenate([jnp.zeros((bsz, 1, width), jnp.float32), jnp.cumsum(zf, axis=1)], axis=1)
    t = np.arange(length)
    outs = []
    for g, w in enumerate(POOL_WINDOWS):
        lo = np.maximum(t - w // 2, 0)
        hi = np.minimum(t + w // 2 - 1, length - 1)
        cnt = jnp.asarray((hi - lo + 1).astype(np.float32))[None, :, None]
        sl = slice(g * POOL_GROUP_DIM, (g + 1) * POOL_GROUP_DIM)
        seg = cs[:, :, sl]
        outs.append((seg[:, hi + 1] - seg[:, lo]) / cnt - zf[:, :, sl])
    return jnp.concatenate(outs, axis=-1).astype(z.dtype)


def pool_branch(z, w, s):
    bsz, length = z.shape[:2]
    pooled = multi_pool(z).reshape(bsz, length, len(POOL_WINDOWS), POOL_GROUP_DIM)
    y = jnp.einsum("blgi,gio->blgo", pooled, w).reshape(bsz, length, POOL_WIDTH)
    return y * s


def neighbourhood_attn(q, k, v, k_ctx, v_ctx, rpb, rows):
    bsz = q.shape[0]
    kh = min(NA_KH_MAX, rows)
    qg = q.reshape(bsz, rows, GRID_W, NA_HEADS, HEAD_DIM)
    kg = k.reshape(bsz, rows, GRID_W, NA_HEADS, HEAD_DIM)
    vg = v.reshape(bsz, rows, GRID_W, NA_HEADS, HEAD_DIM)
    c_idx = np.arange(GRID_W)
    c0 = np.clip(c_idx - NA_KW // 2, 0, GRID_W - NA_KW)
    cols = c0[:, None] + np.arange(NA_KW)[None, :]
    dcol = cols - c_idx[:, None] + NA_KW - 1
    n_nb = kh * NA_KW

    def row_block(r):
        r0 = jnp.clip(r - kh // 2, 0, rows - kh)
        k_band = lax.dynamic_slice_in_dim(kg, r0, kh, axis=1)
        v_band = lax.dynamic_slice_in_dim(vg, r0, kh, axis=1)
        k_nb = k_band[:, :, cols]
        v_nb = v_band[:, :, cols]
        q_r = lax.dynamic_index_in_dim(qg, r, axis=1, keepdims=False)
        s_nb = jnp.einsum("bqhd,bmqnhd->bhqmn", q_r, k_nb).astype(jnp.float32)
        drow = r0 + jnp.arange(kh) - r + NA_KH_MAX - 1
        bias = rpb[:, drow[:, None, None], dcol[None, :, :]]
        s_nb = s_nb + jnp.transpose(bias, (0, 2, 1, 3)).astype(jnp.float32)[None]
        s_ctx = jnp.einsum("bqhd,bchd->bhqc", q_r, k_ctx).astype(jnp.float32)
        s = jnp.concatenate([s_nb.reshape(bsz, NA_HEADS, GRID_W, n_nb), s_ctx], axis=-1)
        p = jax.nn.softmax(s, axis=-1).astype(v.dtype)
        p_nb = p[..., :n_nb].reshape(bsz, NA_HEADS, GRID_W, kh, NA_KW)
        p_ctx = p[..., n_nb:]
        return (jnp.einsum("bhqmn,bmqnhd->bqhd", p_nb, v_nb)
                + jnp.einsum("bhqc,bchd->bqhd", p_ctx, v_ctx))

    o = lax.map(row_block, jnp.arange(rows))
    return jnp.moveaxis(o, 0, 1).reshape(bsz, rows * GRID_W, NA_WIDTH)


def setup_inputs(seed: int = 0) -> dict:
    key = jax.random.key(seed)
    ks = jax.random.split(key, 16)
    f32 = jnp.float32
    n = lambda k, shape: jax.random.normal(k, shape, f32)
    return {
        "x": n(ks[0], (BATCH, SEQ, D_MODEL)),
        "c": n(ks[1], (BATCH, D_MODEL)),
        "ctx": n(ks[2], (BATCH, CTX_LEN, D_MODEL)),
        "c_ctx": n(ks[3], (D_MODEL,)),
        "norm_gain": 1.0 + 0.02 * n(ks[4], (DEPTH, D_MODEL)),
        "w_mod": 0.5 * D_MODEL ** -0.5 * n(ks[5], (DEPTH, D_MODEL, 3 * D_MODEL)),
        "b_mod": 0.02 * n(ks[6], (DEPTH, 3 * D_MODEL)),
        "w_in": D_MODEL ** -0.5 * n(ks[7], (DEPTH, D_MODEL, IN_WIDTH)),
        "att_q_gain": 1.0 + 0.02 * n(ks[8], (DEPTH, HEAD_DIM)),
        "att_k_gain": 1.0 + 0.02 * n(ks[9], (DEPTH, HEAD_DIM)),
        "pool_w": POOL_GROUP_DIM ** -0.5 * n(ks[10], (DEPTH, len(POOL_WINDOWS), POOL_GROUP_DIM, POOL_GROUP_DIM)),
        "pool_scale": 1.0 + 0.02 * n(ks[11], (DEPTH, POOL_WIDTH)),
        "na_q_gain": 1.0 + 0.02 * n(ks[12], (DEPTH, HEAD_DIM)),
        "na_k_gain": 1.0 + 0.02 * n(ks[13], (DEPTH, HEAD_DIM)),
        "na_rpb": 0.1 * n(ks[14], (DEPTH, NA_HEADS, 2 * NA_KH_MAX - 1, 2 * NA_KW - 1)),
        "w_out": MIX_WIDTH ** -0.5 * n(ks[15], (DEPTH, MIX_WIDTH, D_MODEL)),
    }


def reference(x, c, ctx, c_ctx, norm_gain, w_mod, b_mod, w_in, att_q_gain, att_k_gain,
              pool_w, pool_scale, na_q_gain, na_k_gain, na_rpb, w_out):
    bsz, seq, _ = x.shape
    n_ctx = ctx.shape[1]
    rows = seq // GRID_W
    t = jnp.arange(seq)
    row, col = t // GRID_W, t % GRID_W
    silu = jax.nn.silu
    for l in range(DEPTH):
        last = l == DEPTH - 1
        shift, scale, gate = jnp.split(silu(c) @ w_mod[l] + b_mod[l], 3, axis=-1)
        shift_c, scale_c, gate_c = jnp.split(silu(c_ctx) @ w_mod[l] + b_mod[l], 3, axis=-1)
        h = rms_norm(x, norm_gain[l]) * (1 + scale[:, None]) + shift[:, None]
        hc = rms_norm(ctx, norm_gain[l]) * (1 + scale_c) + shift_c

        aq, ak, av, ag, bz, bg, nq, nk, nv, ng = jnp.split(h @ w_in[l], SPLIT_POINTS, axis=-1)
        if last:
            ak_c, av_c = jnp.split(hc @ w_in[l][:, IN_OFFSETS[1]:IN_OFFSETS[3]], 2, axis=-1)
            nk_c, nv_c = jnp.split(hc @ w_in[l][:, IN_OFFSETS[7]:IN_OFFSETS[9]], 2, axis=-1)
        else:
            (aq_c, ak_c, av_c, ag_c, bz_c, bg_c,
             nq_c, nk_c, nv_c, ng_c) = jnp.split(hc @ w_in[l], SPLIT_POINTS, axis=-1)

        ak_c = rms_norm(ak_c.reshape(bsz, n_ctx, ATT_KV_HEADS, HEAD_DIM), att_k_gain[l])
        av_c = av_c.reshape(bsz, n_ctx, ATT_KV_HEADS, HEAD_DIM)
        nk_c = rms_norm(nk_c.reshape(bsz, n_ctx, NA_HEADS, HEAD_DIM), na_k_gain[l])
        nv_c = nv_c.reshape(bsz, n_ctx, NA_HEADS, HEAD_DIM)

        aq = rope_2d(rms_norm(aq.reshape(bsz, seq, ATT_HEADS, HEAD_DIM), att_q_gain[l]), row, col) * ATTN_SCALE
        aq = aq.reshape(bsz, seq, ATT_KV_HEADS, ATT_GROUP, HEAD_DIM)
        ak = rope_2d(rms_norm(ak.reshape(bsz, seq, ATT_KV_HEADS, HEAD_DIM), att_k_gain[l]), row, col)
        av = av.reshape(bsz, seq, ATT_KV_HEADS, HEAD_DIM)
        a_out = global_gqa(aq, ak, av, ak_c, av_c)
        b_out = pool_branch(bz, pool_w[l], pool_scale[l])
        nq = rms_norm(nq.reshape(bsz, seq, NA_HEADS, HEAD_DIM), na_q_gain[l]) * ATTN_SCALE
        nk = rms_norm(nk.reshape(bsz, seq, NA_HEADS, HEAD_DIM), na_k_gain[l])
        nv = nv.reshape(bsz, seq, NA_HEADS, HEAD_DIM)
        n_out = neighbourhood_attn(nq, nk, nv, nk_c, nv_c, na_rpb[l], rows)

        y = jnp.concatenate([a_out * silu(ag), b_out * silu(bg), n_out * silu(ng)], axis=-1) @ w_out[l]

        if not last:
            aq_c = rms_norm(aq_c.reshape(bsz, n_ctx, ATT_HEADS, HEAD_DIM), att_q_gain[l]) * ATTN_SCALE
            a_out_c = dense_attn(aq_c.reshape(bsz, n_ctx, ATT_KV_HEADS, ATT_GROUP, HEAD_DIM), ak_c, av_c)
            b_out_c = pool_branch(bz_c, pool_w[l], pool_scale[l])
            nq_c = rms_norm(nq_c.reshape(bsz, n_ctx, NA_HEADS, HEAD_DIM), na_q_gain[l]) * ATTN_SCALE
            n_out_c = dense_attn(nq_c[:, :, :, None, :], nk_c, nv_c)
            yc = jnp.concatenate([a_out_c * silu(ag_c), b_out_c * silu(bg_c), n_out_c * silu(ng_c)],
                                 axis=-1) @ w_out[l]
            ctx = ctx + gate_c * yc

        x = x + gate[:, None] * y
    return x
```

```python
import functools

import numpy as np
import jax
import jax.numpy as jnp
from jax import lax
from jax.experimental import pallas as pl
from jax.experimental.pallas import tpu as pltpu

F32 = jnp.float32
BF16 = jnp.bfloat16

D_MODEL = 1024
DEPTH = 4
GRID_W = 64
HEAD_DIM = 64
ROPE_THETA = 10000.0
EPS = 1e-6
ATTN_SCALE = HEAD_DIM ** -0.5
POOL_WINDOWS = (2, 4, 8, 16)
NA_KH = 8
NA_KW = 16
IN_WIDTH = 2816

LANES = 128
GROUP_W = 256
HEADS_PER_GROUP = GROUP_W // HEAD_DIM
KV_CHUNK = 256

W_AQ, W_AK, W_AV, W_AG, W_BZ, W_BG, W_NQ, W_NK, W_NV, W_NG = (
    0, 512, 640, 768, 1280, 1536, 1792, 2048, 2304, 2560)
P_QA, P_KE, P_VE, P_GA, P_BZ, P_GB, P_NQ, P_NK, P_NV, P_GN = (
    0, 512, 1024, 1536, 2048, 2304, 2560, 2816, 3072, 3328)
P_WIDTH = 3584
MOD_ROWS = 24
NEG = -1e30
VMEM_LIMIT = 48 * 1024 * 1024


def _silu(v):
    return v / (1.0 + jnp.exp(-v))


def _split_bf16(v):
    hi = v.astype(BF16)
    lo = (v - hi.astype(F32)).astype(BF16)
    return hi, lo


def _dot(a, b):
    return jnp.dot(a, b, preferred_element_type=F32)


def _mod_kernel(c_ref, w_ref, b_ref, o_ref):
    a_hi, a_lo = _split_bf16(_silu(c_ref[...]))
    w_hi, w_lo = _split_bf16(w_ref[0])
    o_ref[0] = _dot(a_hi, w_hi) + _dot(a_lo, w_hi) + _dot(a_hi, w_lo) + b_ref[0]


def _modulation(c, c_ctx, w_mod, b_mod):
    bsz = c.shape[0]
    cs = jnp.concatenate(
        [c, c_ctx[None], jnp.zeros((MOD_ROWS - bsz - 1, D_MODEL), F32)], axis=0)
    return pl.pallas_call(
        _mod_kernel,
        grid=(DEPTH, 3),
        in_specs=[
            pl.BlockSpec((MOD_ROWS, D_MODEL), lambda l, j: (0, 0)),
            pl.BlockSpec((1, D_MODEL, D_MODEL), lambda l, j: (l, 0, j)),
            pl.BlockSpec((1, 1, D_MODEL), lambda l, j: (l, 0, j)),
        ],
        out_specs=pl.BlockSpec((1, MOD_ROWS, D_MODEL), lambda l, j: (l, 0, j)),
        out_shape=jax.ShapeDtypeStruct((DEPTH, MOD_ROWS, 3 * D_MODEL), F32),
        compiler_params=pltpu.CompilerParams(vmem_limit_bytes=VMEM_LIMIT),
        name="modulation",
    )(cs, w_mod, b_mod.reshape(DEPTH, 1, 3 * D_MODEL))


def _proj_kernel(x_ref, shift_ref, scale_ref, ng_ref, w_ref, cos_ref, sa_ref, sb_ref,
                 aqg_ref, akg_ref, nqg_ref, nkg_ref, seg_ref, rep_ref, o_ref, hb_ref):
    x = x_ref[0]
    ms = jnp.mean(x * x, axis=-1, keepdims=True)
    h = x * lax.rsqrt(ms + EPS) * ng_ref[...]
    h = h * (1.0 + scale_ref[0]) + shift_ref[0]
    hb_ref[...] = h.astype(BF16)

    def proj(c0, width):
        return _dot(hb_ref[...], w_ref[:, c0:c0 + width])

    def head_norm(y, g):
        hi, lo = _split_bf16(y * y)
        ss = _dot(hi, seg_ref[...]) + _dot(lo, seg_ref[...])
        return y * lax.rsqrt(ss * (1.0 / HEAD_DIM) + EPS) * g

    def rope(y):
        return (y * cos_ref[...] + pltpu.roll(y, LANES - 16, 1) * sa_ref[...]
                + pltpu.roll(y, 16, 1) * sb_ref[...])

    def put(c0, v):
        o_ref[0, :, c0:c0 + v.shape[-1]] = v.astype(BF16)

    for j in range(4):
        y = proj(W_AQ + j * LANES, LANES)
        put(P_QA + j * LANES, rope(head_norm(y, aqg_ref[...])) * ATTN_SCALE)
    kb = rope(head_norm(proj(W_AK, LANES), akg_ref[...])).astype(BF16)
    put(P_KE, _dot(kb, rep_ref[...]))
    put(P_VE, _dot(proj(W_AV, LANES).astype(BF16), rep_ref[...]))
    put(P_GA, _silu(proj(W_AG, 512)))
    put(P_BZ, proj(W_BZ, 256))
    put(P_GB, _silu(proj(W_BG, 256)))
    for j in range(2):
        y = proj(W_NQ + j * LANES, LANES)
        put(P_NQ + j * LANES, head_norm(y, nqg_ref[...]) * ATTN_SCALE)
        y = proj(W_NK + j * LANES, LANES)
        put(P_NK + j * LANES, head_norm(y, nkg_ref[...]))
    put(P_NV, proj(W_NV, 256))
    put(P_GN, _silu(proj(W_NG, 256)))


def _projection(x, mods, mod_row, norm_gain, w_in, tables, gains, consts, tm):
    bsz, length, _ = x.shape
    cos, sa, sb = tables
    seg, rep = consts
    vec = lambda k: pl.BlockSpec((1, 1, D_MODEL), lambda b, i: (mod_row(b), 0, k))
    const2 = lambda a: pl.BlockSpec(a.shape, lambda b, i: (0, 0))
    tab = pl.BlockSpec((tm, LANES), lambda b, i: (i, 0))
    return pl.pallas_call(
        _proj_kernel,
        grid=(bsz, length // tm),
        in_specs=[
            pl.BlockSpec((1, tm, D_MODEL), lambda b, i: (b, i, 0)),
            vec(0), vec(1), const2(norm_gain), const2(w_in), tab, tab, tab,
            const2(gains[0]), const2(gains[1]), const2(gains[2]), const2(gains[3]),
            const2(seg), const2(rep),
        ],
        out_specs=pl.BlockSpec((1, tm, P_WIDTH), lambda b, i: (b, i, 0)),
        out_shape=jax.ShapeDtypeStruct((bsz, length, P_WIDTH), BF16),
        scratch_shapes=[pltpu.VMEM((tm, D_MODEL), BF16)],
        compiler_params=pltpu.CompilerParams(vmem_limit_bytes=VMEM_LIMIT),
        name="projection",
    )(x, mods, mods, norm_gain, w_in, cos, sa, sb, *gains, seg, rep)


def _attn_kernel(*refs, tq, n_kv, chunks, n_bias):
    q_ref, g_ref = refs[0], refs[1]
    kv = refs[2:2 + 2 * n_kv]
    pos = 2 + 2 * n_kv
    bias_ref = refs[pos] if n_bias else None
    pos += 1 if n_bias else 0
    o_ref, qs_ref, s_ref = refs[pos:pos + 3]

    lane_head = jnp.right_shift(lax.broadcasted_iota(jnp.int32, (tq, GROUP_W), 1), 6)
    q = q_ref[0].astype(F32)
    for h in range(HEADS_PER_GROUP):
        qs_ref[h * tq:(h + 1) * tq, :] = jnp.where(lane_head == h, q, 0.0).astype(BF16)
    qs = qs_ref[...]

    m = None
    for ci, (ai, off) in enumerate(chunks):
        k = kv[2 * ai][0, off:off + KV_CHUNK, :]
        s = lax.dot_general(qs, k, (((1,), (1,)), ((), ())), preferred_element_type=F32)
        if ci < n_bias:
            s = s + bias_ref[0, :, ci * KV_CHUNK:(ci + 1) * KV_CHUNK]
        s_ref[ci] = s
        m = s if m is None else jnp.maximum(m, s)
    m_row = jnp.max(m, axis=-1, keepdims=True)

    l = jnp.zeros((HEADS_PER_GROUP * tq, KV_CHUNK), F32)
    acc = jnp.zeros((HEADS_PER_GROUP * tq, GROUP_W), F32)
    for ci, (ai, off) in enumerate(chunks):
        p = jnp.exp(s_ref[ci] - m_row)
        l = l + p
        acc = acc + _dot(p.astype(BF16), kv[2 * ai + 1][0, off:off + KV_CHUNK, :])
    o = acc / jnp.sum(l, axis=-1, keepdims=True)
    out = jnp.zeros((tq, GROUP_W), F32)
    for h in range(HEADS_PER_GROUP):
        out = out + jnp.where(lane_head == h, o[h * tq:(h + 1) * tq, :], 0.0)
    o_ref[0] = (out * g_ref[0].astype(F32)).astype(BF16)


def _attention(grid, q, gate, kvs, bias, out_shape, out_map, tq, chunks, name):
    n_kv = len(kvs) // 2
    n_bias = 0
    operands = [q[0], gate[0]]
    specs = [pl.BlockSpec((1, tq, GROUP_W), q[1]), pl.BlockSpec((1, tq, GROUP_W), gate[1])]
    for arr, rows, imap in kvs:
        operands.append(arr)
        specs.append(pl.BlockSpec((1, rows, GROUP_W), imap))
    if bias is not None:
        arr, imap = bias
        n_bias = arr.shape[-1] // KV_CHUNK
        operands.append(arr)
        specs.append(pl.BlockSpec((1,) + arr.shape[1:], imap))
    rows = HEADS_PER_GROUP * tq
    return pl.pallas_call(
        functools.partial(_attn_kernel, tq=tq, n_kv=n_kv, chunks=chunks, n_bias=n_bias),
        grid=grid,
        in_specs=specs,
        out_specs=pl.BlockSpec((1, tq, GROUP_W), out_map),
        out_shape=jax.ShapeDtypeStruct(out_shape, BF16),
        scratch_shapes=[pltpu.VMEM((rows, GROUP_W), BF16),
                        pltpu.VMEM((len(chunks), rows, KV_CHUNK), F32)],
        compiler_params=pltpu.CompilerParams(vmem_limit_bytes=VMEM_LIMIT),
        name=name,
    )(*operands)


def _col(off):
    return off // GROUP_W


def _gqa(p, p_ctx, tq, with_latent_keys):
    bsz, length, _ = p.shape
    kvs, chunks = [], []
    if with_latent_keys:
        kvs += [(p, length, lambda b, j, i: (b, 0, _col(P_KE) + j)),
                (p, length, lambda b, j, i: (b, 0, _col(P_VE) + j))]
        chunks += [(0, c * KV_CHUNK) for c in range(length // KV_CHUNK)]
    n_ctx = p_ctx.shape[1]
    kvs += [(p_ctx, n_ctx, lambda b, j, i: (b, 0, _col(P_KE) + j)),
            (p_ctx, n_ctx, lambda b, j, i: (b, 0, _col(P_VE) + j))]
    chunks += [(len(kvs) // 2 - 1, c * KV_CHUNK) for c in range(n_ctx // KV_CHUNK)]
    return _attention(
        grid=(bsz, 2, length // tq),
        q=(p, lambda b, j, i: (b, i, _col(P_QA) + j)),
        gate=(p, lambda b, j, i: (b, i, _col(P_GA) + j)),
        kvs=kvs, bias=None,
        out_shape=(bsz, length, 2 * GROUP_W), out_map=lambda b, j, i: (b, i, j),
        tq=tq, chunks=tuple(chunks), name="gqa_attention")


NA_TQ = 256
NA_BAND = 3
NA_GROUPS = 2048 // NA_TQ


def _na_band_start(g):
    return jnp.clip(g - 1, 0, NA_GROUPS - NA_BAND)


def _na_latent(p, p_ctx, bias):
    bsz, length, _ = p.shape
    kvs = []
    for i in range(NA_BAND):
        kvs.append((p, KV_CHUNK, lambda g, b, i=i: (b, _na_band_start(g) + i, _col(P_NK))))
        kvs.append((p, KV_CHUNK, lambda g, b, i=i: (b, _na_band_start(g) + i, _col(P_NV))))
    n_ctx = p_ctx.shape[1]
    kvs += [(p_ctx, n_ctx, lambda g, b: (b, 0, _col(P_NK))),
            (p_ctx, n_ctx, lambda g, b: (b, 0, _col(P_NV)))]
    chunks = tuple((i, 0) for i in range(NA_BAND)) + tuple(
        (NA_BAND, c * KV_CHUNK) for c in range(n_ctx // KV_CHUNK))
    pattern = lambda g, b: (jnp.where(g == 0, 0, jnp.where(g == NA_GROUPS - 1, 2, 1)), 0, 0)
    return _attention(
        grid=(NA_GROUPS, bsz),
        q=(p, lambda g, b: (b, g, _col(P_NQ))),
        gate=(p, lambda g, b: (b, g, _col(P_GN))),
        kvs=kvs, bias=(bias, pattern),
        out_shape=(bsz, length, GROUP_W), out_map=lambda g, b: (b, g, 0),
        tq=NA_TQ, chunks=chunks, name="na_attention")


def _na_context(p_ctx):
    bsz, n_ctx, _ = p_ctx.shape
    kvs = [(p_ctx, n_ctx, lambda b, i: (b, 0, _col(P_NK))),
           (p_ctx, n_ctx, lambda b, i: (b, 0, _col(P_NV)))]
    return _attention(
        grid=(bsz, n_ctx // NA_TQ),
        q=(p_ctx, lambda b, i: (b, i, _col(P_NQ))),
        gate=(p_ctx, lambda b, i: (b, i, _col(P_GN))),
        kvs=kvs, bias=None,
        out_shape=(bsz, n_ctx, GROUP_W), out_map=lambda b, i: (b, i, 0),
        tq=NA_TQ, chunks=tuple((0, c * KV_CHUNK) for c in range(n_ctx // KV_CHUNK)),
        name="na_context_attention")


def _na_bias_tables():
    drow = np.zeros((3, NA_TQ, NA_BAND * KV_CHUNK), np.int32)
    dcol = np.zeros_like(drow)
    valid = np.zeros(drow.shape, bool)
    rows = 2048 // GRID_W
    for pat, g in enumerate((0, 1, NA_GROUPS - 1)):
        band0 = (KV_CHUNK // GRID_W) * min(max(g - 1, 0), NA_GROUPS - NA_BAND)
        qi = np.arange(NA_TQ)
        r, c = (NA_TQ // GRID_W) * g + qi // GRID_W, qi % GRID_W
        kk = np.arange(NA_BAND * KV_CHUNK)
        rp, cp = band0 + kk // GRID_W, kk % GRID_W
        r0 = np.clip(r - NA_KH // 2, 0, rows - NA_KH)
        c0 = np.clip(c - NA_KW // 2, 0, GRID_W - NA_KW)
        vr = (rp[None] >= r0[:, None]) & (rp[None] < r0[:, None] + NA_KH)
        vc = (cp[None] >= c0[:, None]) & (cp[None] < c0[:, None] + NA_KW)
        valid[pat] = vr & vc
        drow[pat] = np.clip(rp[None] - r[:, None] + NA_KH - 1, 0, 2 * NA_KH - 2)
        dcol[pat] = np.clip(cp[None] - c[:, None] + NA_KW - 1, 0, 2 * NA_KW - 2)
    return drow, dcol, valid


POOL_HALO = 8
POOL_ROWS = 256


def _pool_kernel(z_ref, g_ref, cnt_ref, w_ref, s_ref, o_ref, zp_ref, *, length):
    zeros = jnp.zeros((POOL_HALO, GROUP_W), F32)
    zp_ref[0:POOL_HALO, :] = zeros
    zp_ref[POOL_HALO + length:2 * POOL_HALO + length, :] = zeros
    zp_ref[POOL_HALO:POOL_HALO + length, :] = z_ref[0].astype(F32)
    first = lax.broadcasted_iota(jnp.int32, (POOL_ROWS, LANES), 1) < HEAD_DIM

    def window(t0, lanes, offsets):
        acc = None
        for o in offsets:
            v = zp_ref[POOL_HALO + t0 + o:POOL_HALO + t0 + o + POOL_ROWS, lanes]
            acc = v if acc is None else acc + v
        return acc

    for t0 in range(0, length, POOL_ROWS):
        lo, hi = slice(0, LANES), slice(LANES, 2 * LANES)
        s2 = window(t0, lo, (-1, 0))
        s4 = s2 + window(t0, lo, (-2, 1))
        s8 = window(t0, hi, range(-4, 4))
        s16 = s8 + window(t0, hi, tuple(range(-8, -4)) + tuple(range(4, 8)))
        win = jnp.concatenate([jnp.where(first, s2, s4), jnp.where(first, s8, s16)], axis=-1)
        rows = slice(t0, t0 + POOL_ROWS)
        z = zp_ref[POOL_HALO + t0:POOL_HALO + t0 + POOL_ROWS, :]
        pooled = win / cnt_ref[rows, :] - z
        y = _dot(pooled.astype(BF16), w_ref[...]) * s_ref[...]
        o_ref[0, rows, :] = (y * g_ref[0, rows, :].astype(F32)).astype(BF16)


def _pool(p, cnt, w_bd, scale):
    bsz, length, _ = p.shape
    return pl.pallas_call(
        functools.partial(_pool_kernel, length=length),
        grid=(bsz,),
        in_specs=[
            pl.BlockSpec((1, length, GROUP_W), lambda b: (b, 0, _col(P_BZ))),
            pl.BlockSpec((1, length, GROUP_W), lambda b: (b, 0, _col(P_GB))),
            pl.BlockSpec((length, GROUP_W), lambda b: (0, 0)),
            pl.BlockSpec((GROUP_W, GROUP_W), lambda b: (0, 0)),
            pl.BlockSpec((1, GROUP_W), lambda b: (0, 0)),
        ],
        out_specs=pl.BlockSpec((1, length, GROUP_W), lambda b: (b, 0, 0)),
        out_shape=jax.ShapeDtypeStruct((bsz, length, GROUP_W), BF16),
        scratch_shapes=[pltpu.VMEM((length + 2 * POOL_HALO, GROUP_W), F32)],
        compiler_params=pltpu.CompilerParams(vmem_limit_bytes=VMEM_LIMIT),
        name="pool_mixer",
    )(p, p, cnt, w_bd, scale)


def _pool_counts(length):
    t = np.arange(length)
    cols = []
    for w in POOL_WINDOWS:
        lo = np.maximum(t - w // 2, 0)
        hi = np.minimum(t + w // 2 - 1, length - 1)
        cols.append(np.repeat((hi - lo + 1).astype(np.float32)[:, None], HEAD_DIM, axis=1))
    return np.concatenate(cols, axis=1)


def _out_kernel(x_ref, gate_ref, a_ref, b_ref, n_ref, w_ref, o_ref):
    y = (_dot(a_ref[0], w_ref[0:512, :]) + _dot(b_ref[0], w_ref[512:768, :])
         + _dot(n_ref[0], w_ref[768:1024, :]))
    o_ref[0] = x_ref[0] + gate_ref[0] * y


def _output(x, mods, mod_row, a, b, n, w_out, tm):
    bsz, length, _ = x.shape
    tok = lambda width: pl.BlockSpec((1, tm, width), lambda bb, i: (bb, i, 0))
    return pl.pallas_call(
        _out_kernel,
        grid=(bsz, length // tm),
        in_specs=[
            tok(D_MODEL),
            pl.BlockSpec((1, 1, D_MODEL), lambda bb, i: (mod_row(bb), 0, 2)),
            tok(2 * GROUP_W), tok(GROUP_W), tok(GROUP_W),
            pl.BlockSpec((D_MODEL, D_MODEL), lambda bb, i: (0, 0)),
        ],
        out_specs=tok(D_MODEL),
        out_shape=jax.ShapeDtypeStruct(x.shape, F32),
        compiler_params=pltpu.CompilerParams(vmem_limit_bytes=VMEM_LIMIT),
        name="output_projection",
    )(x, mods, a, b, n, w_out)


def _rope_tables(seq):
    t = jnp.arange(seq)
    half = HEAD_DIM // 4
    inv_freq = ROPE_THETA ** (-jnp.arange(half, dtype=jnp.float32) / half)
    cos, sa, sb = [], [], []
    zero = jnp.zeros((seq, half), F32)
    for pos in (t // GRID_W, t % GRID_W):
        ang = pos.astype(jnp.float32)[:, None] * inv_freq[None, :]
        cos += [jnp.cos(ang), jnp.cos(ang)]
        sa += [-jnp.sin(ang), zero]
        sb += [zero, jnp.sin(ang)]
    tile = lambda parts: jnp.tile(jnp.concatenate(parts, axis=-1), (1, LANES // HEAD_DIM))
    return tile(cos), tile(sa), tile(sb)


def kernel(x, c, ctx, c_ctx, norm_gain, w_mod, b_mod, w_in, att_q_gain, att_k_gain,
           pool_w, pool_scale, na_q_gain, na_k_gain, na_rpb, w_out):
    bsz, seq, _ = x.shape
    n_ctx = ctx.shape[1]
    assert seq == NA_GROUPS * NA_TQ and n_ctx % KV_CHUNK == 0 and bsz < MOD_ROWS

    mods_all = _modulation(c, c_ctx, w_mod, b_mod)
    tables = _rope_tables(seq)
    tables_ctx = (jnp.ones((n_ctx, LANES), F32), jnp.zeros((n_ctx, LANES), F32),
                  jnp.zeros((n_ctx, LANES), F32))
    lane = np.arange(LANES)
    seg = jnp.asarray(lane[:, None] // HEAD_DIM == lane[None, :] // HEAD_DIM, BF16)
    col = np.arange(2 * GROUP_W)
    rep = jnp.asarray(lane[:, None] == (col[None, :] // GROUP_W) * HEAD_DIM
                      + col[None, :] % HEAD_DIM, BF16)
    cnt, cnt_ctx = jnp.asarray(_pool_counts(seq)), jnp.asarray(_pool_counts(n_ctx))
    drow, dcol, valid = _na_bias_tables()
    w_in_b = w_in.astype(BF16)
    w_out_b = w_out.astype(BF16)
    two = lambda g: jnp.tile(g, LANES // HEAD_DIM)[None, :]
    lat_row = lambda b: b
    ctx_row = lambda b: bsz

    for l in range(DEPTH):
        last = l == DEPTH - 1
        mods = mods_all[l].reshape(MOD_ROWS, 1, 3 * D_MODEL)
        gains = (two(att_q_gain[l]), two(att_k_gain[l]), two(na_q_gain[l]), two(na_k_gain[l]))
        ng = norm_gain[l][None, :]
        p = _projection(x, mods, lat_row, ng, w_in_b[l], tables, gains, (seg, rep), 256)
        p_ctx = _projection(ctx, mods, ctx_row, ng, w_in_b[l], tables_ctx, gains,
                            (seg, rep), 256)

        bias = jnp.where(valid[None], na_rpb[l][:, drow, dcol], NEG)
        bias = jnp.transpose(bias, (1, 0, 2, 3)).reshape(
            3, HEADS_PER_GROUP * NA_TQ, NA_BAND * KV_CHUNK)
        w_bd = jax.scipy.linalg.block_diag(*[pool_w[l, g] for g in range(4)]).astype(BF16)
        ps = pool_scale[l][None, :]

        a = _gqa(p, p_ctx, 128, True)
        n = _na_latent(p, p_ctx, bias)
        bo = _pool(p, cnt, w_bd, ps)
        x_new = _output(x, mods, lat_row, a, bo, n, w_out_b[l], 256)

        if not last:
            a_c = _gqa(p_ctx, p_ctx, 128, False)
            n_c = _na_context(p_ctx)
            bo_c = _pool(p_ctx, cnt_ctx, w_bd, ps)
            ctx = _output(ctx, mods, ctx_row, a_c, bo_c, n_c, w_out_b[l], 256)
        x = x_new
    return x
```

```python
import functools

import numpy as np
import jax
import jax.numpy as jnp
from jax import lax
from jax.experimental import pallas as pl
from jax.experimental.pallas import tpu as pltpu

F32 = jnp.float32
BF16 = jnp.bfloat16

D_MODEL = 1024
DEPTH = 4
GRID_W = 64
HEAD_DIM = 64
ROPE_THETA = 10000.0
EPS = 1e-6
ATTN_SCALE = HEAD_DIM ** -0.5
POOL_WINDOWS = (2, 4, 8, 16)
NA_KH = 8
NA_KW = 16
IN_WIDTH = 2816

LANES = 128
GROUP_W = 256
HEADS_PER_GROUP = GROUP_W // HEAD_DIM
KV_CHUNK = 256

W_AQ, W_AK, W_AV, W_AG, W_BZ, W_BG, W_NQ, W_NK, W_NV, W_NG = (
    0, 512, 640, 768, 1280, 1536, 1792, 2048, 2304, 2560)
P_QA, P_KE, P_VE, P_GA, P_BZ, P_GB, P_NQ, P_NK, P_NV, P_GN = (
    0, 512, 1024, 1536, 2048, 2304, 2560, 2816, 3072, 3328)
P_WIDTH = 3584
MOD_ROWS = 24
NEG = -1e30
VMEM_LIMIT = 48 * 1024 * 1024


def _silu(v):
    return v / (1.0 + jnp.exp(-v))


def _split_bf16(v):
    hi = v.astype(BF16)
    lo = (v - hi.astype(F32)).astype(BF16)
    return hi, lo


def _dot(a, b):
    return jnp.dot(a, b, preferred_element_type=F32)


def _mod_kernel(c_ref, w_ref, b_ref, o_ref):
    a_hi, a_lo = _split_bf16(_silu(c_ref[...]))
    w_hi, w_lo = _split_bf16(w_ref[0])
    o_ref[0] = _dot(a_hi, w_hi) + _dot(a_lo, w_hi) + _dot(a_hi, w_lo) + b_ref[0]


def _modulation(c, c_ctx, w_mod, b_mod):
    bsz = c.shape[0]
    cs = jnp.concatenate(
        [c, c_ctx[None], jnp.zeros((MOD_ROWS - bsz - 1, D_MODEL), F32)], axis=0)
    return pl.pallas_call(
        _mod_kernel,
        grid=(DEPTH, 3),
        in_specs=[
            pl.BlockSpec((MOD_ROWS, D_MODEL), lambda l, j: (0, 0)),
            pl.BlockSpec((1, D_MODEL, D_MODEL), lambda l, j: (l, 0, j)),
            pl.BlockSpec((1, 1, D_MODEL), lambda l, j: (l, 0, j)),
        ],
        out_specs=pl.BlockSpec((1, MOD_ROWS, D_MODEL), lambda l, j: (l, 0, j)),
        out_shape=jax.ShapeDtypeStruct((DEPTH, MOD_ROWS, 3 * D_MODEL), F32),
        compiler_params=pltpu.CompilerParams(vmem_limit_bytes=VMEM_LIMIT),
        name="modulation",
    )(cs, w_mod, b_mod.reshape(DEPTH, 1, 3 * D_MODEL))


def _proj_kernel(x_ref, shift_ref, scale_ref, ng_ref, w_ref, cos_ref, sa_ref, sb_ref,
                 aqg_ref, akg_ref, nqg_ref, nkg_ref, seg_ref, rep_ref, o_ref, hb_ref):
    x = x_ref[0]
    ms = jnp.mean(x * x, axis=-1, keepdims=True)
    h = x * lax.rsqrt(ms + EPS) * ng_ref[...]
    h = h * (1.0 + scale_ref[0]) + shift_ref[0]
    hb_ref[...] = h.astype(BF16)

    def proj(c0, width):
        return _dot(hb_ref[...], w_ref[:, c0:c0 + width])

    def head_norm(y, g):
        ss = _dot((y * y).astype(BF16), seg_ref[...])
        return y * lax.rsqrt(ss * (1.0 / HEAD_DIM) + EPS) * g

    def rope(y):
        width = y.shape[-1]
        wide = lambda t: jnp.tile(t[...], (1, width // LANES))
        return (y * wide(cos_ref) + pltpu.roll(y, width - 16, 1) * wide(sa_ref)
                + pltpu.roll(y, 16, 1) * wide(sb_ref))

    def put(c0, v):
        o_ref[0, :, c0:c0 + v.shape[-1]] = v.astype(BF16)

    for j in range(2):
        y = proj(W_AQ + j * GROUP_W, GROUP_W)
        put(P_QA + j * GROUP_W, rope(head_norm(y, aqg_ref[...])) * ATTN_SCALE)
    kv = proj(W_AK, GROUP_W)
    kb = rope(head_norm(kv, akg_ref[...]))[:, :LANES].astype(BF16)
    put(P_KE, _dot(kb, rep_ref[...]))
    put(P_VE, _dot(kv[:, LANES:].astype(BF16), rep_ref[...]))
    put(P_GA, _silu(proj(W_AG, 512)))
    put(P_BZ, proj(W_BZ, 256))
    put(P_GB, _silu(proj(W_BG, 256)))
    put(P_NQ, head_norm(proj(W_NQ, GROUP_W), nqg_ref[...]) * ATTN_SCALE)
    put(P_NK, head_norm(proj(W_NK, GROUP_W), nkg_ref[...]))
    put(P_NV, proj(W_NV, 256))
    put(P_GN, _silu(proj(W_NG, 256)))


def _projection(x, mods, mod_row, norm_gain, w_in, tables, gains, consts, tm):
    bsz, length, _ = x.shape
    cos, sa, sb = tables
    seg, rep = consts
    vec = lambda k: pl.BlockSpec((1, 1, D_MODEL), lambda b, i: (mod_row(b), 0, k))
    const2 = lambda a: pl.BlockSpec(a.shape, lambda b, i: (0, 0))
    tab = pl.BlockSpec((tm, LANES), lambda b, i: (i, 0))
    return pl.pallas_call(
        _proj_kernel,
        grid=(bsz, length // tm),
        in_specs=[
            pl.BlockSpec((1, tm, D_MODEL), lambda b, i: (b, i, 0)),
            vec(0), vec(1), const2(norm_gain), const2(w_in), tab, tab, tab,
            const2(gains[0]), const2(gains[1]), const2(gains[2]), const2(gains[3]),
            const2(seg), const2(rep),
        ],
        out_specs=pl.BlockSpec((1, tm, P_WIDTH), lambda b, i: (b, i, 0)),
        out_shape=jax.ShapeDtypeStruct((bsz, length, P_WIDTH), BF16),
        scratch_shapes=[pltpu.VMEM((tm, D_MODEL), BF16)],
        compiler_params=pltpu.CompilerParams(vmem_limit_bytes=VMEM_LIMIT),
        name="projection",
    )(x, mods, mods, norm_gain, w_in, cos, sa, sb, *gains, seg, rep)


def _attn_kernel(*refs, tq, tiles_per_row, n_tiles, n_kv, chunks, n_bias):
    q_ref, g_ref = refs[0], refs[1]
    kv = refs[2:2 + 2 * n_kv]
    pos = 2 + 2 * n_kv
    bias_ref = refs[pos] if n_bias else None
    pos += 1 if n_bias else 0
    o_ref, qs_ref, s_a, s_b, m_a, m_b = refs[pos:pos + 6]
    rows = HEADS_PER_GROUP * tq
    lane_head = jnp.right_shift(lax.broadcasted_iota(jnp.int32, (tq, GROUP_W), 1), 6)
    shift = tiles_per_row.bit_length() - 1

    def tile_pos(t):
        if tiles_per_row == 1:
            return t, 0
        r0 = (t & (tiles_per_row - 1)) * tq
        return t >> shift, (r0 if isinstance(r0, int) else pl.multiple_of(r0, tq))

    def stage(t1, s1, m1, t0, s0, m0):
        if t1 is not None:
            b1, r1 = tile_pos(t1)
            q = q_ref[b1, pl.ds(r1, tq), :].astype(F32)
            for h in range(HEADS_PER_GROUP):
                qs_ref[h * tq:(h + 1) * tq, :] = jnp.where(lane_head == h, q, 0.0).astype(BF16)
            qs = qs_ref[...]
            m = None
        if t0 is not None:
            b0, r0 = tile_pos(t0)
            m_prev = m0[...]
            m_prev = jnp.concatenate([m_prev, m_prev], axis=-1)
            l = jnp.zeros((rows, LANES), F32)
            acc = jnp.zeros((rows, GROUP_W), F32)
        for ci, (ai, off) in enumerate(chunks):
            if t1 is not None:
                k = kv[2 * ai][b1, off:off + KV_CHUNK, :]
                s = lax.dot_general(qs, k, (((1,), (1,)), ((), ())),
                                    preferred_element_type=F32)
                if ci < n_bias:
                    s = s + bias_ref[0, :, ci * KV_CHUNK:(ci + 1) * KV_CHUNK]
                s1[ci] = s
                m = s if m is None else jnp.maximum(m, s)
            if t0 is not None:
                p = jnp.exp(s0[ci] - m_prev)
                l = l + (p[:, :LANES] + p[:, LANES:])
                acc = acc + _dot(p.astype(BF16), kv[2 * ai + 1][b0, off:off + KV_CHUNK, :])
        if t1 is not None:
            m1[...] = jnp.broadcast_to(jnp.max(m, axis=-1, keepdims=True), (rows, LANES))
        if t0 is not None:
            o = acc / jnp.sum(l, axis=-1, keepdims=True)
            out = jnp.zeros((tq, GROUP_W), F32)
            for h in range(HEADS_PER_GROUP):
                out = out + jnp.where(lane_head == h, o[h * tq:(h + 1) * tq, :], 0.0)
            gate = g_ref[b0, pl.ds(r0, tq), :].astype(F32)
            o_ref[b0, pl.ds(r0, tq), :] = (out * gate).astype(BF16)

    stage(0, s_a, m_a, None, None, None)

    def body(j, carry):
        stage(2 * j + 1, s_b, m_b, 2 * j, s_a, m_a)
        stage(jnp.minimum(2 * j + 2, n_tiles - 1), s_a, m_a, 2 * j + 1, s_b, m_b)
        return carry

    lax.fori_loop(0, n_tiles // 2, body, 0)


def _attention(grid, nb, q_rows, q, gate, kvs, bias, out_shape, out_map, tq, chunks, name):
    n_kv = len(kvs) // 2
    n_bias = 0
    operands = [q[0], gate[0]]
    specs = [pl.BlockSpec((nb, q_rows, GROUP_W), q[1]),
             pl.BlockSpec((nb, q_rows, GROUP_W), gate[1])]
    for arr, kv_rows, imap in kvs:
        operands.append(arr)
        specs.append(pl.BlockSpec((nb, kv_rows, GROUP_W), imap))
    if bias is not None:
        arr, imap = bias
        n_bias = arr.shape[-1] // KV_CHUNK
        operands.append(arr)
        specs.append(pl.BlockSpec((1,) + arr.shape[1:], imap))
    rows = HEADS_PER_GROUP * tq
    tiles_per_row = q_rows // tq
    n_tiles = nb * tiles_per_row
    assert n_tiles % 2 == 0 and tiles_per_row & (tiles_per_row - 1) == 0
    score_buf = pltpu.VMEM((len(chunks), rows, KV_CHUNK), F32)
    max_buf = pltpu.VMEM((rows, LANES), F32)
    return pl.pallas_call(
        functools.partial(_attn_kernel, tq=tq, tiles_per_row=tiles_per_row, n_tiles=n_tiles,
                          n_kv=n_kv, chunks=chunks, n_bias=n_bias),
        grid=grid,
        in_specs=specs,
        out_specs=pl.BlockSpec((nb, q_rows, GROUP_W), out_map),
        out_shape=jax.ShapeDtypeStruct(out_shape, BF16),
        scratch_shapes=[pltpu.VMEM((rows, GROUP_W), BF16), score_buf, score_buf,
                        max_buf, max_buf],
        compiler_params=pltpu.CompilerParams(vmem_limit_bytes=VMEM_LIMIT),
        name=name,
    )(*operands)


def _col(off):
    return off // GROUP_W


GQA_TQ = 128
CTX_NB = 8


def _gqa_latent(p, p_ctx):
    bsz, length, _ = p.shape
    n_ctx = p_ctx.shape[1]
    kvs = [(p, length, lambda b, j: (b, 0, _col(P_KE) + j)),
           (p, length, lambda b, j: (b, 0, _col(P_VE) + j)),
           (p_ctx, n_ctx, lambda b, j: (b, 0, _col(P_KE) + j)),
           (p_ctx, n_ctx, lambda b, j: (b, 0, _col(P_VE) + j))]
    chunks = tuple((0, c * KV_CHUNK) for c in range(length // KV_CHUNK)) + tuple(
        (1, c * KV_CHUNK) for c in range(n_ctx // KV_CHUNK))
    return _attention(
        grid=(bsz, 2), nb=1, q_rows=length,
        q=(p, lambda b, j: (b, 0, _col(P_QA) + j)),
        gate=(p, lambda b, j: (b, 0, _col(P_GA) + j)),
        kvs=kvs, bias=None,
        out_shape=(bsz, length, 2 * GROUP_W), out_map=lambda b, j: (b, 0, j),
        tq=GQA_TQ, chunks=chunks, name="gqa_attention")


def _gqa_context(p_ctx):
    bsz, n_ctx, _ = p_ctx.shape
    kvs = [(p_ctx, n_ctx, lambda b, j: (b, 0, _col(P_KE) + j)),
           (p_ctx, n_ctx, lambda b, j: (b, 0, _col(P_VE) + j))]
    return _attention(
        grid=(bsz // CTX_NB, 2), nb=CTX_NB, q_rows=n_ctx,
        q=(p_ctx, lambda b, j: (b, 0, _col(P_QA) + j)),
        gate=(p_ctx, lambda b, j: (b, 0, _col(P_GA) + j)),
        kvs=kvs, bias=None,
        out_shape=(bsz, n_ctx, 2 * GROUP_W), out_map=lambda b, j: (b, 0, j),
        tq=GQA_TQ, chunks=tuple((0, c * KV_CHUNK) for c in range(n_ctx // KV_CHUNK)),
        name="gqa_context_attention")


NA_TQ = 256
NA_BAND = 3
NA_GROUPS = 2048 // NA_TQ
NA_NB = 4
NA_PATTERN_GROUPS = (0, 1, NA_GROUPS - 1)


def _na_band_start(g):
    return jnp.clip(g - 1, 0, NA_GROUPS - NA_BAND)


def _na_latent(p, p_ctx, bias):
    bsz, length, _ = p.shape
    kvs = []
    for i in range(NA_BAND):
        kvs.append((p, KV_CHUNK, lambda g, b, i=i: (b, _na_band_start(g) + i, _col(P_NK))))
        kvs.append((p, KV_CHUNK, lambda g, b, i=i: (b, _na_band_start(g) + i, _col(P_NV))))
    n_ctx = p_ctx.shape[1]
    kvs += [(p_ctx, n_ctx, lambda g, b: (b, 0, _col(P_NK))),
            (p_ctx, n_ctx, lambda g, b: (b, 0, _col(P_NV)))]
    chunks = tuple((i, 0) for i in range(NA_BAND)) + tuple(
        (NA_BAND, c * KV_CHUNK) for c in range(n_ctx // KV_CHUNK))
    pattern = lambda g, b: (jnp.where(g == 0, 0, jnp.where(g == NA_GROUPS - 1, 2, 1)), 0, 0)
    return _attention(
        grid=(NA_GROUPS, bsz // NA_NB), nb=NA_NB, q_rows=NA_TQ,
        q=(p, lambda g, b: (b, g, _col(P_NQ))),
        gate=(p, lambda g, b: (b, g, _col(P_GN))),
        kvs=kvs, bias=(bias, pattern),
        out_shape=(bsz, length, GROUP_W), out_map=lambda g, b: (b, g, 0),
        tq=NA_TQ, chunks=chunks, name="na_attention")


def _na_context(p_ctx):
    bsz, n_ctx, _ = p_ctx.shape
    kvs = [(p_ctx, n_ctx, lambda b: (b, 0, _col(P_NK))),
           (p_ctx, n_ctx, lambda b: (b, 0, _col(P_NV)))]
    return _attention(
        grid=(bsz // CTX_NB,), nb=CTX_NB, q_rows=n_ctx,
        q=(p_ctx, lambda b: (b, 0, _col(P_NQ))),
        gate=(p_ctx, lambda b: (b, 0, _col(P_GN))),
        kvs=kvs, bias=None,
        out_shape=(bsz, n_ctx, GROUP_W), out_map=lambda b: (b, 0, 0),
        tq=NA_TQ, chunks=tuple((0, c * KV_CHUNK) for c in range(n_ctx // KV_CHUNK)),
        name="na_context_attention")


def _na_bias_layout():
    rows = 2048 // GRID_W
    rows_per_tile, rows_per_chunk = NA_TQ // GRID_W, KV_CHUNK // GRID_W
    band_rows = NA_BAND * rows_per_chunk
    slot = np.full((len(NA_PATTERN_GROUPS), rows_per_tile, band_rows), 2 * NA_KH - 1)
    for pat, g in enumerate(NA_PATTERN_GROUPS):
        band0 = rows_per_chunk * min(max(g - 1, 0), NA_GROUPS - NA_BAND)
        for qr in range(rows_per_tile):
            r = rows_per_tile * g + qr
            r0 = min(max(r - NA_KH // 2, 0), rows - NA_KH)
            for kr in range(band_rows):
                rp = band0 + kr
                if r0 <= rp < r0 + NA_KH:
                    slot[pat, qr, kr] = rp - r + NA_KH - 1
    c = np.arange(GRID_W)
    c0 = np.clip(c - NA_KW // 2, 0, GRID_W - NA_KW)
    col_ok = (c[None, :] >= c0[:, None]) & (c[None, :] < c0[:, None] + NA_KW)
    return slot, col_ok


def _na_bias(rpb, slot, col_ok):
    heads, n_dr, _ = rpb.shape
    w = jnp.concatenate([rpb[..., NA_KW - 1:],
                         jnp.full((heads, n_dr, LANES - (2 * NA_KW - 1)), NEG, F32),
                         rpb[..., :NA_KW - 1]], axis=-1)
    t = jnp.tile(w, (1, 1, GRID_W))[..., :GRID_W * (LANES - 1)]
    t = t.reshape(heads, n_dr, GRID_W, LANES - 1)[..., :GRID_W]
    t = jnp.where(col_ok[None, None], t, NEG)
    t = jnp.concatenate([t, jnp.full((heads, 1, GRID_W, GRID_W), NEG, F32)], axis=1)
    n_pat, n_qr, n_kr = slot.shape
    blocks = jnp.stack([t[:, int(s)] for s in slot.reshape(-1)], axis=1)
    blocks = blocks.reshape(heads, n_pat, n_qr, n_kr, GRID_W, GRID_W)
    blocks = jnp.transpose(blocks, (1, 0, 2, 4, 3, 5))
    return blocks.reshape(n_pat, heads * n_qr * GRID_W, n_kr * GRID_W)


POOL_HALO = 8
POOL_ROWS = 256


def _pool_kernel(z_ref, g_ref, cnt_ref, w_ref, s_ref, o_ref, zp_ref, *, length):
    zeros = jnp.zeros((POOL_HALO, GROUP_W), F32)
    zp_ref[0:POOL_HALO, :] = zeros
    zp_ref[POOL_HALO + length:2 * POOL_HALO + length, :] = zeros
    zp_ref[POOL_HALO:POOL_HALO + length, :] = z_ref[0].astype(F32)
    first = lax.broadcasted_iota(jnp.int32, (POOL_ROWS, LANES), 1) < HEAD_DIM

    def window(t0, lanes, offsets):
        acc = None
        for o in offsets:
            v = zp_ref[POOL_HALO + t0 + o:POOL_HALO + t0 + o + POOL_ROWS, lanes]
            acc = v if acc is None else acc + v
        return acc

    for t0 in range(0, length, POOL_ROWS):
        lo, hi = slice(0, LANES), slice(LANES, 2 * LANES)
        s2 = window(t0, lo, (-1, 0))
        s4 = s2 + window(t0, lo, (-2, 1))
        s8 = window(t0, hi, range(-4, 4))
        s16 = s8 + window(t0, hi, tuple(range(-8, -4)) + tuple(range(4, 8)))
        win = jnp.concatenate([jnp.where(first, s2, s4), jnp.where(first, s8, s16)], axis=-1)
        rows = slice(t0, t0 + POOL_ROWS)
        z = zp_ref[POOL_HALO + t0:POOL_HALO + t0 + POOL_ROWS, :]
        pooled = win / cnt_ref[rows, :] - z
        y = _dot(pooled.astype(BF16), w_ref[...]) * s_ref[...]
        o_ref[0, rows, :] = (y * g_ref[0, rows, :].astype(F32)).astype(BF16)


def _pool(p, cnt, w_bd, scale):
    bsz, length, _ = p.shape
    return pl.pallas_call(
        functools.partial(_pool_kernel, length=length),
        grid=(bsz,),
        in_specs=[
            pl.BlockSpec((1, length, GROUP_W), lambda b: (b, 0, _col(P_BZ))),
            pl.BlockSpec((1, length, GROUP_W), lambda b: (b, 0, _col(P_GB))),
            pl.BlockSpec((length, GROUP_W), lambda b: (0, 0)),
            pl.BlockSpec((GROUP_W, GROUP_W), lambda b: (0, 0)),
            pl.BlockSpec((1, GROUP_W), lambda b: (0, 0)),
        ],
        out_specs=pl.BlockSpec((1, length, GROUP_W), lambda b: (b, 0, 0)),
        out_shape=jax.ShapeDtypeStruct((bsz, length, GROUP_W), BF16),
        scratch_shapes=[pltpu.VMEM((length + 2 * POOL_HALO, GROUP_W), F32)],
        compiler_params=pltpu.CompilerParams(vmem_limit_bytes=VMEM_LIMIT),
        name="pool_mixer",
    )(p, p, cnt, w_bd, scale)


def _pool_counts(length):
    t = np.arange(length)
    cols = []
    for w in POOL_WINDOWS:
        lo = np.maximum(t - w // 2, 0)
        hi = np.minimum(t + w // 2 - 1, length - 1)
        cols.append(np.repeat((hi - lo + 1).astype(np.float32)[:, None], HEAD_DIM, axis=1))
    return np.concatenate(cols, axis=1)


def _out_kernel(x_ref, gate_ref, a_ref, b_ref, n_ref, w_ref, o_ref):
    y = (_dot(a_ref[0], w_ref[0:512, :]) + _dot(b_ref[0], w_ref[512:768, :])
         + _dot(n_ref[0], w_ref[768:1024, :]))
    o_ref[0] = x_ref[0] + gate_ref[0] * y


def _output(x, mods, mod_row, a, b, n, w_out, tm):
    bsz, length, _ = x.shape
    tok = lambda width: pl.BlockSpec((1, tm, width), lambda bb, i: (bb, i, 0))
    return pl.pallas_call(
        _out_kernel,
        grid=(bsz, length // tm),
        in_specs=[
            tok(D_MODEL),
            pl.BlockSpec((1, 1, D_MODEL), lambda bb, i: (mod_row(bb), 0, 2)),
            tok(2 * GROUP_W), tok(GROUP_W), tok(GROUP_W),
            pl.BlockSpec((D_MODEL, D_MODEL), lambda bb, i: (0, 0)),
        ],
        out_specs=tok(D_MODEL),
        out_shape=jax.ShapeDtypeStruct(x.shape, F32),
        compiler_params=pltpu.CompilerParams(vmem_limit_bytes=VMEM_LIMIT),
        name="output_projection",
    )(x, mods, a, b, n, w_out)


def _rope_tables(seq):
    t = jnp.arange(seq)
    half = HEAD_DIM // 4
    inv_freq = ROPE_THETA ** (-jnp.arange(half, dtype=jnp.float32) / half)
    cos, sa, sb = [], [], []
    zero = jnp.zeros((seq, half), F32)
    for pos in (t // GRID_W, t % GRID_W):
        ang = pos.astype(jnp.float32)[:, None] * inv_freq[None, :]
        cos += [jnp.cos(ang), jnp.cos(ang)]
        sa += [-jnp.sin(ang), zero]
        sb += [zero, jnp.sin(ang)]
    tile = lambda parts: jnp.tile(jnp.concatenate(parts, axis=-1), (1, LANES // HEAD_DIM))
    return tile(cos), tile(sa), tile(sb)


def kernel(x, c, ctx, c_ctx, norm_gain, w_mod, b_mod, w_in, att_q_gain, att_k_gain,
           pool_w, pool_scale, na_q_gain, na_k_gain, na_rpb, w_out):
    bsz, seq, _ = x.shape
    n_ctx = ctx.shape[1]
    assert seq == NA_GROUPS * NA_TQ and n_ctx % KV_CHUNK == 0 and bsz < MOD_ROWS
    assert bsz % CTX_NB == 0 and bsz % NA_NB == 0

    mods_all = _modulation(c, c_ctx, w_mod, b_mod)
    tables = _rope_tables(seq)
    tables_ctx = (jnp.ones((n_ctx, LANES), F32), jnp.zeros((n_ctx, LANES), F32),
                  jnp.zeros((n_ctx, LANES), F32))
    lane = np.arange(GROUP_W)
    seg = jnp.asarray(lane[:, None] // HEAD_DIM == lane[None, :] // HEAD_DIM, BF16)
    col = np.arange(2 * GROUP_W)
    rep = jnp.asarray(np.arange(LANES)[:, None] == (col[None, :] // GROUP_W) * HEAD_DIM
                      + col[None, :] % HEAD_DIM, BF16)
    cnt, cnt_ctx = jnp.asarray(_pool_counts(seq)), jnp.asarray(_pool_counts(n_ctx))
    slot, col_ok = _na_bias_layout()
    w_in_b = w_in.astype(BF16)
    w_out_b = w_out.astype(BF16)
    wide = lambda g: jnp.tile(g, GROUP_W // HEAD_DIM)[None, :]
    lat_row = lambda b: b
    ctx_row = lambda b: bsz

    for l in range(DEPTH):
        last = l == DEPTH - 1
        mods = mods_all[l].reshape(MOD_ROWS, 1, 3 * D_MODEL)
        gains = (wide(att_q_gain[l]), wide(att_k_gain[l]), wide(na_q_gain[l]),
                 wide(na_k_gain[l]))
        ng = norm_gain[l][None, :]
        p = _projection(x, mods, lat_row, ng, w_in_b[l], tables, gains, (seg, rep), 256)
        p_ctx = _projection(ctx, mods, ctx_row, ng, w_in_b[l], tables_ctx, gains,
                            (seg, rep), 256)
        bias = _na_bias(na_rpb[l], slot, col_ok)
        w_bd = jax.scipy.linalg.block_diag(*[pool_w[l, g] for g in range(4)]).astype(BF16)
        ps = pool_scale[l][None, :]

        a = _gqa_latent(p, p_ctx)
        n = _na_latent(p, p_ctx, bias)
        bo = _pool(p, cnt, w_bd, ps)
        x_new = _output(x, mods, lat_row, a, bo, n, w_out_b[l], 256)

        if not last:
            a_c = _gqa_context(p_ctx)
            n_c = _na_context(p_ctx)
            bo_c = _pool(p_ctx, cnt_ctx, w_bd, ps)
            ctx = _output(ctx, mods, ctx_row, a_c, bo_c, n_c, w_out_b[l], 256)
        x = x_new
    return x
```

```python
import functools
import math

import numpy as np
import jax
import jax.numpy as jnp
from jax import lax
from jax.experimental import pallas as pl
from jax.experimental.pallas import tpu as pltpu

F32 = jnp.float32
BF16 = jnp.bfloat16

D_MODEL = 1024
DEPTH = 4
GRID_W = 64
HEAD_DIM = 64
ROPE_THETA = 10000.0
EPS = 1e-6
ATTN_SCALE = HEAD_DIM ** -0.5
LOG2E = 1.4426950408889634
Q_SCALE = ATTN_SCALE * LOG2E
POOL_WINDOWS = (2, 4, 8, 16)
NA_KH = 8
NA_KW = 16
IN_WIDTH = 2816

LANES = 128
GROUP_W = 256
HEADS_PER_GROUP = GROUP_W // HEAD_DIM
KV_CHUNK = 256
TILES_PER_ITER = 8

W_AQ, W_AK, W_AV, W_AG, W_BZ, W_BG, W_NQ, W_NK, W_NV, W_NG = (
    0, 512, 640, 768, 1280, 1536, 1792, 2048, 2304, 2560)
P_QA, P_KE, P_VE, P_GA, P_BZ, P_GB, P_NQ, P_NK, P_NV, P_GN = (
    0, 512, 1024, 1280, 1792, 2048, 2304, 2560, 2816, 3072)
P_WIDTH = 3328
MOD_ROWS = 24
TOKEN_TILE = 512
NEG = -1e30
VMEM_LIMIT = 48 * 1024 * 1024


def _silu(v):
    return v / (1.0 + jnp.exp(-v))


def _split_bf16(v):
    hi = v.astype(BF16)
    lo = (v - hi.astype(F32)).astype(BF16)
    return hi, lo


def _dot(a, b):
    return jnp.dot(a, b, preferred_element_type=F32)


def _mod_kernel(c_ref, w_ref, b_ref, o_ref):
    a_hi, a_lo = _split_bf16(_silu(c_ref[...]))
    w_hi, w_lo = _split_bf16(w_ref[0])
    o_ref[0] = _dot(a_hi, w_hi) + _dot(a_lo, w_hi) + _dot(a_hi, w_lo) + b_ref[0]


def _modulation(c, c_ctx, w_mod, b_mod):
    bsz = c.shape[0]
    cs = jnp.concatenate(
        [c, c_ctx[None], jnp.zeros((MOD_ROWS - bsz - 1, D_MODEL), F32)], axis=0)
    return pl.pallas_call(
        _mod_kernel,
        grid=(DEPTH, 3),
        in_specs=[
            pl.BlockSpec((MOD_ROWS, D_MODEL), lambda l, j: (0, 0)),
            pl.BlockSpec((1, D_MODEL, D_MODEL), lambda l, j: (l, 0, j)),
            pl.BlockSpec((1, 1, D_MODEL), lambda l, j: (l, 0, j)),
        ],
        out_specs=pl.BlockSpec((1, MOD_ROWS, D_MODEL), lambda l, j: (l, 0, j)),
        out_shape=jax.ShapeDtypeStruct((DEPTH, MOD_ROWS, 3 * D_MODEL), F32),
        compiler_params=pltpu.CompilerParams(vmem_limit_bytes=VMEM_LIMIT),
        name="modulation",
    )(cs, w_mod, b_mod.reshape(DEPTH, 1, 3 * D_MODEL))


def _proj_kernel(x_ref, shift_ref, scale_ref, ng_ref, w_ref, cos_ref, sa_ref, sb_ref,
                 aqg_ref, akg_ref, nqg_ref, nkg_ref, seg_ref, repk_ref, repv_ref, ones_ref,
                 o_ref, hb_ref):
    x = x_ref[0]
    ms = jnp.mean(x * x, axis=-1, keepdims=True)
    h = x * lax.rsqrt(ms + EPS) * ng_ref[...]
    h = h * (1.0 + scale_ref[0]) + shift_ref[0]
    hb_ref[...] = h.astype(BF16)

    def proj(c0, width):
        return _dot(hb_ref[...], w_ref[:, c0:c0 + width])

    def head_norm(y, g):
        ss = _dot((y * y).astype(BF16), seg_ref[...])
        return y * lax.rsqrt(ss * (1.0 / HEAD_DIM) + EPS) * g

    def rope(y):
        width = y.shape[-1]
        wide = lambda t: jnp.tile(t[...], (1, width // LANES))
        return (y * wide(cos_ref) + pltpu.roll(y, width - 16, 1) * wide(sa_ref)
                + pltpu.roll(y, 16, 1) * wide(sb_ref))

    def put(c0, v):
        o_ref[0, :, c0:c0 + v.shape[-1]] = v.astype(BF16)

    for j in range(2):
        y = proj(W_AQ + j * GROUP_W, GROUP_W)
        put(P_QA + j * GROUP_W, rope(head_norm(y, aqg_ref[...])) * Q_SCALE)
    kv = proj(W_AK, GROUP_W)
    kb = rope(head_norm(kv, akg_ref[...]))[:, :LANES].astype(BF16)
    put(P_KE, _dot(kb, repk_ref[...]))
    put(P_VE, _dot(kv[:, LANES:].astype(BF16), repv_ref[...]) + ones_ref[...])
    for j in range(2):
        put(P_GA + j * GROUP_W, _silu(proj(W_AG + j * GROUP_W, GROUP_W)))
    put(P_BZ, proj(W_BZ, 256))
    put(P_GB, _silu(proj(W_BG, 256)))
    put(P_NQ, head_norm(proj(W_NQ, GROUP_W), nqg_ref[...]) * Q_SCALE)
    put(P_NK, head_norm(proj(W_NK, GROUP_W), nkg_ref[...]))
    put(P_NV, proj(W_NV, 256))
    put(P_GN, _silu(proj(W_NG, 256)))


def _projection(x, mods, mod_row, norm_gain, w_in, tables, gains, consts, tm):
    bsz, length, _ = x.shape
    cos, sa, sb = tables
    vec = lambda k: pl.BlockSpec((1, 1, D_MODEL), lambda b, i: (mod_row(b), 0, k))
    const2 = lambda a: pl.BlockSpec(a.shape, lambda b, i: (0, 0))
    tab = pl.BlockSpec((tm, LANES), lambda b, i: (i, 0))
    return pl.pallas_call(
        _proj_kernel,
        grid=(bsz, length // tm),
        in_specs=[
            pl.BlockSpec((1, tm, D_MODEL), lambda b, i: (b, i, 0)),
            vec(0), vec(1), const2(norm_gain), const2(w_in), tab, tab, tab,
            *[const2(g) for g in gains], *[const2(a) for a in consts],
        ],
        out_specs=pl.BlockSpec((1, tm, P_WIDTH), lambda b, i: (b, i, 0)),
        out_shape=jax.ShapeDtypeStruct((bsz, length, P_WIDTH), BF16),
        scratch_shapes=[pltpu.VMEM((tm, D_MODEL), BF16)],
        compiler_params=pltpu.CompilerParams(vmem_limit_bytes=VMEM_LIMIT),
        name="projection",
    )(x, mods, mods, norm_gain, w_in, cos, sa, sb, *gains, *consts)


def _attn_kernel(*refs, tq, tiles_per_row, n_tiles, tiles_per_iter, n_kv, chunks, n_bias,
                 sums_in_v):
    q_ref, g_ref = refs[0], refs[1]
    kv = refs[2:2 + 2 * n_kv]
    pos = 2 + 2 * n_kv
    bias_ref = refs[pos] if n_bias else None
    pos += 1 if n_bias else 0
    o_ref, s_a, s_b, m_a, m_b, mc_ref, lc_ref = refs[pos:pos + 7]
    rows = HEADS_PER_GROUP * tq
    n_chunks = len(chunks)
    lane_head = jnp.right_shift(lax.broadcasted_iota(jnp.int32, (tq, GROUP_W), 1), 6)
    low_half = lax.broadcasted_iota(jnp.int32, (tq, LANES), 1) < HEAD_DIM
    shift = tiles_per_row.bit_length() - 1

    def tile_pos(t):
        if tiles_per_row == 1:
            return t, 0
        r0 = (t & (tiles_per_row - 1)) * tq
        return t >> shift, (r0 if isinstance(r0, int) else pl.multiple_of(r0, tq))

    def stage(t1, s1, m1, t0, s0, m0):
        if t1 is not None:
            b1, r1 = tile_pos(t1)
            q = q_ref[b1, pl.ds(r1, tq), :].astype(F32)
            qs = jnp.concatenate([jnp.where(lane_head == h, q, 0.0).astype(BF16)
                                  for h in range(HEADS_PER_GROUP)], axis=0)
        if t0 is not None:
            b0, r0 = tile_pos(t0)
            m_prev = m0[...]
            m_prev = jnp.concatenate([m_prev, m_prev], axis=-1)
            acc = None
        for ci, (ai, off) in enumerate(chunks):
            if t1 is not None:
                k = kv[2 * ai][b1, off:off + KV_CHUNK, :]
                s = lax.dot_general(qs, k, (((1,), (1,)), ((), ())),
                                    preferred_element_type=F32)
                if ci < n_bias:
                    s = s + bias_ref[0, :, ci * KV_CHUNK:(ci + 1) * KV_CHUNK]
                s1[ci] = s
                mc_ref[ci] = jnp.maximum(s[:, :LANES], s[:, LANES:])
            if t0 is not None:
                p = jnp.exp2(s0[ci] - m_prev)
                if not sums_in_v:
                    lc_ref[ci] = p[:, :LANES] + p[:, LANES:]
                d = _dot(p.astype(BF16), kv[2 * ai + 1][b0, off:off + KV_CHUNK, :])
                acc = d if acc is None else acc + d
        if t1 is not None:
            m = mc_ref[0]
            for ci in range(1, n_chunks):
                m = jnp.maximum(m, mc_ref[ci])
            m1[...] = jnp.broadcast_to(jnp.max(m, axis=-1, keepdims=True), (rows, LANES))
        if t0 is not None:
            gate = g_ref[b0, pl.ds(r0, tq), :].astype(F32)
            head = lambda a, h: a[h * tq:(h + 1) * tq, :]
            if sums_in_v:
                o = acc / pltpu.roll(acc, HEAD_DIM, 1)
                halves = [jnp.where(low_half, head(o, h), pltpu.roll(head(o, h + 1), HEAD_DIM, 1))
                          for h in (0, 2)]
                out = jnp.concatenate(halves, axis=-1)
            else:
                l = lc_ref[0]
                for ci in range(1, n_chunks):
                    l = l + lc_ref[ci]
                o = acc / jnp.sum(l, axis=-1, keepdims=True)
                out = jnp.zeros((tq, GROUP_W), F32)
                for h in range(HEADS_PER_GROUP):
                    out = out + jnp.where(lane_head == h, head(o, h), 0.0)
            o_ref[b0, pl.ds(r0, tq), :] = (out * gate).astype(BF16)

    stage(0, s_a, m_a, None, None, None)

    def body(j, carry):
        t = tiles_per_iter * j
        for u in range(0, tiles_per_iter, 2):
            stage(t + u + 1, s_b, m_b, t + u, s_a, m_a)
            stage(jnp.minimum(t + u + 2, n_tiles - 1), s_a, m_a, t + u + 1, s_b, m_b)
        return carry

    lax.fori_loop(0, n_tiles // tiles_per_iter, body, 0)


def _attention(grid, nb, q_rows, q, gate, kvs, bias, out_shape, out_map, tq, chunks, name):
    n_kv = len(kvs) // 2
    n_bias = 0
    operands = [q[0], gate[0]]
    specs = [pl.BlockSpec((nb, q_rows, GROUP_W), q[1]),
             pl.BlockSpec((nb, q_rows, GROUP_W), gate[1])]
    for arr, kv_rows, kv_lanes, imap in kvs:
        operands.append(arr)
        specs.append(pl.BlockSpec((nb, kv_rows, kv_lanes), imap))
    sums_in_v = kvs[1][2] == LANES
    if bias is not None:
        arr, imap = bias
        n_bias = arr.shape[-1] // KV_CHUNK
        operands.append(arr)
        specs.append(pl.BlockSpec((1,) + arr.shape[1:], imap))
    rows = HEADS_PER_GROUP * tq
    tiles_per_row = q_rows // tq
    n_tiles = nb * tiles_per_row
    assert n_tiles % 2 == 0 and tiles_per_row & (tiles_per_row - 1) == 0
    score_buf = pltpu.VMEM((len(chunks), rows, KV_CHUNK), F32)
    max_buf = pltpu.VMEM((rows, LANES), F32)
    part_buf = pltpu.VMEM((len(chunks), rows, LANES), F32)
    sum_buf = pltpu.VMEM((1, 8, LANES) if sums_in_v else (len(chunks), rows, LANES), F32)
    return pl.pallas_call(
        functools.partial(_attn_kernel, tq=tq, tiles_per_row=tiles_per_row, n_tiles=n_tiles,
                          tiles_per_iter=math.gcd(n_tiles, TILES_PER_ITER), n_kv=n_kv,
                          chunks=chunks, n_bias=n_bias, sums_in_v=sums_in_v),
        grid=grid,
        in_specs=specs,
        out_specs=pl.BlockSpec((nb, q_rows, GROUP_W), out_map),
        out_shape=jax.ShapeDtypeStruct(out_shape, BF16),
        scratch_shapes=[score_buf, score_buf, max_buf, max_buf, part_buf, sum_buf],
        compiler_params=pltpu.CompilerParams(vmem_limit_bytes=VMEM_LIMIT),
        name=name,
    )(*operands)


def _col(off):
    return off // GROUP_W


GQA_TQ = 128
CTX_NB = 8


def _gqa_kv(p):
    rows = p.shape[1]
    return [(p, rows, GROUP_W, lambda b, j: (b, 0, _col(P_KE) + j)),
            (p, rows, LANES, lambda b, j: (b, 0, P_VE // LANES + j))]


def _gqa_latent(p, p_ctx):
    bsz, length, _ = p.shape
    n_ctx = p_ctx.shape[1]
    kvs = _gqa_kv(p) + _gqa_kv(p_ctx)
    chunks = tuple((0, c * KV_CHUNK) for c in range(length // KV_CHUNK)) + tuple(
        (1, c * KV_CHUNK) for c in range(n_ctx // KV_CHUNK))
    return _attention(
        grid=(bsz, 2), nb=1, q_rows=length,
        q=(p, lambda b, j: (b, 0, _col(P_QA) + j)),
        gate=(p, lambda b, j: (b, 0, _col(P_GA) + j)),
        kvs=kvs, bias=None,
        out_shape=(bsz, length, 2 * GROUP_W), out_map=lambda b, j: (b, 0, j),
        tq=GQA_TQ, chunks=chunks, name="gqa_attention")


def _gqa_context(p_ctx):
    bsz, n_ctx, _ = p_ctx.shape
    return _attention(
        grid=(bsz // CTX_NB, 2), nb=CTX_NB, q_rows=n_ctx,
        q=(p_ctx, lambda b, j: (b, 0, _col(P_QA) + j)),
        gate=(p_ctx, lambda b, j: (b, 0, _col(P_GA) + j)),
        kvs=_gqa_kv(p_ctx), bias=None,
        out_shape=(bsz, n_ctx, 2 * GROUP_W), out_map=lambda b, j: (b, 0, j),
        tq=GQA_TQ, chunks=tuple((0, c * KV_CHUNK) for c in range(n_ctx // KV_CHUNK)),
        name="gqa_context_attention")


NA_TQ = 256
NA_BAND = 3
NA_GROUPS = 2048 // NA_TQ
NA_NB = 4
NA_PATTERN_GROUPS = (0, 1, NA_GROUPS - 1)


def _na_band_start(g):
    return jnp.clip(g - 1, 0, NA_GROUPS - NA_BAND)


def _na_latent(p, p_ctx, bias):
    bsz, length, _ = p.shape
    kvs = []
    for i in range(NA_BAND):
        kvs.append((p, KV_CHUNK, GROUP_W,
                    lambda g, b, i=i: (b, _na_band_start(g) + i, _col(P_NK))))
        kvs.append((p, KV_CHUNK, GROUP_W,
                    lambda g, b, i=i: (b, _na_band_start(g) + i, _col(P_NV))))
    n_ctx = p_ctx.shape[1]
    kvs += [(p_ctx, n_ctx, GROUP_W, lambda g, b: (b, 0, _col(P_NK))),
            (p_ctx, n_ctx, GROUP_W, lambda g, b: (b, 0, _col(P_NV)))]
    chunks = tuple((i, 0) for i in range(NA_BAND)) + tuple(
        (NA_BAND, c * KV_CHUNK) for c in range(n_ctx // KV_CHUNK))
    pattern = lambda g, b: (jnp.where(g == 0, 0, jnp.where(g == NA_GROUPS - 1, 2, 1)), 0, 0)
    return _attention(
        grid=(NA_GROUPS, bsz // NA_NB), nb=NA_NB, q_rows=NA_TQ,
        q=(p, lambda g, b: (b, g, _col(P_NQ))),
        gate=(p, lambda g, b: (b, g, _col(P_GN))),
        kvs=kvs, bias=(bias, pattern),
        out_shape=(bsz, length, GROUP_W), out_map=lambda g, b: (b, g, 0),
        tq=NA_TQ, chunks=chunks, name="na_attention")


def _na_context(p_ctx):
    bsz, n_ctx, _ = p_ctx.shape
    kvs = [(p_ctx, n_ctx, GROUP_W, lambda b: (b, 0, _col(P_NK))),
           (p_ctx, n_ctx, GROUP_W, lambda b: (b, 0, _col(P_NV)))]
    return _attention(
        grid=(bsz // CTX_NB,), nb=CTX_NB, q_rows=n_ctx,
        q=(p_ctx, lambda b: (b, 0, _col(P_NQ))),
        gate=(p_ctx, lambda b: (b, 0, _col(P_GN))),
        kvs=kvs, bias=None,
        out_shape=(bsz, n_ctx, GROUP_W), out_map=lambda b: (b, 0, 0),
        tq=NA_TQ, chunks=tuple((0, c * KV_CHUNK) for c in range(n_ctx // KV_CHUNK)),
        name="na_context_attention")


def _na_bias_layout():
    rows = 2048 // GRID_W
    rows_per_tile, rows_per_chunk = NA_TQ // GRID_W, KV_CHUNK // GRID_W
    band_rows = NA_BAND * rows_per_chunk
    slot = np.full((len(NA_PATTERN_GROUPS), rows_per_tile, band_rows), 2 * NA_KH - 1)
    for pat, g in enumerate(NA_PATTERN_GROUPS):
        band0 = rows_per_chunk * min(max(g - 1, 0), NA_GROUPS - NA_BAND)
        for qr in range(rows_per_tile):
            r = rows_per_tile * g + qr
            r0 = min(max(r - NA_KH // 2, 0), rows - NA_KH)
            for kr in range(band_rows):
                rp = band0 + kr
                if r0 <= rp < r0 + NA_KH:
                    slot[pat, qr, kr] = rp - r + NA_KH - 1
    c = np.arange(GRID_W)
    c0 = np.clip(c - NA_KW // 2, 0, GRID_W - NA_KW)
    col_ok = (c[None, :] >= c0[:, None]) & (c[None, :] < c0[:, None] + NA_KW)
    return slot, col_ok


def _na_bias(rpb, slot, col_ok):
    heads, n_dr, _ = rpb.shape
    rpb = rpb * LOG2E
    w = jnp.concatenate([rpb[..., NA_KW - 1:],
                         jnp.full((heads, n_dr, LANES - (2 * NA_KW - 1)), NEG, F32),
                         rpb[..., :NA_KW - 1]], axis=-1)
    t = jnp.tile(w, (1, 1, GRID_W))[..., :GRID_W * (LANES - 1)]
    t = t.reshape(heads, n_dr, GRID_W, LANES - 1)[..., :GRID_W]
    t = jnp.where(col_ok[None, None], t, NEG)
    t = jnp.concatenate([t, jnp.full((heads, 1, GRID_W, GRID_W), NEG, F32)], axis=1)
    n_pat, n_qr, n_kr = slot.shape
    blocks = jnp.stack([t[:, int(s)] for s in slot.reshape(-1)], axis=1)
    blocks = blocks.reshape(heads, n_pat, n_qr, n_kr, GRID_W, GRID_W)
    blocks = jnp.transpose(blocks, (1, 0, 2, 4, 3, 5))
    return blocks.reshape(n_pat, heads * n_qr * GRID_W, n_kr * GRID_W)


POOL_HALO = 8
POOL_ROWS = 256


def _pool_kernel(z_ref, g_ref, cnt_ref, w_ref, s_ref, o_ref, zp_ref, *, length):
    zeros = jnp.zeros((POOL_HALO, GROUP_W), F32)
    zp_ref[0:POOL_HALO, :] = zeros
    zp_ref[POOL_HALO + length:2 * POOL_HALO + length, :] = zeros
    zp_ref[POOL_HALO:POOL_HALO + length, :] = z_ref[0].astype(F32)
    first = lax.broadcasted_iota(jnp.int32, (POOL_ROWS, LANES), 1) < HEAD_DIM

    def window(t0, lanes, offsets):
        acc = None
        for o in offsets:
            v = zp_ref[POOL_HALO + t0 + o:POOL_HALO + t0 + o + POOL_ROWS, lanes]
            acc = v if acc is None else acc + v
        return acc

    for t0 in range(0, length, POOL_ROWS):
        lo, hi = slice(0, LANES), slice(LANES, 2 * LANES)
        s2 = window(t0, lo, (-1, 0))
        s4 = s2 + window(t0, lo, (-2, 1))
        s8 = window(t0, hi, range(-4, 4))
        s16 = s8 + window(t0, hi, tuple(range(-8, -4)) + tuple(range(4, 8)))
        win = jnp.concatenate([jnp.where(first, s2, s4), jnp.where(first, s8, s16)], axis=-1)
        rows = slice(t0, t0 + POOL_ROWS)
        z = zp_ref[POOL_HALO + t0:POOL_HALO + t0 + POOL_ROWS, :]
        pooled = win / cnt_ref[rows, :] - z
        y = _dot(pooled.astype(BF16), w_ref[...]) * s_ref[...]
        o_ref[0, rows, :] = (y * g_ref[0, rows, :].astype(F32)).astype(BF16)


def _pool(p, cnt, w_bd, scale):
    bsz, length, _ = p.shape
    return pl.pallas_call(
        functools.partial(_pool_kernel, length=length),
        grid=(bsz,),
        in_specs=[
            pl.BlockSpec((1, length, GROUP_W), lambda b: (b, 0, _col(P_BZ))),
            pl.BlockSpec((1, length, GROUP_W), lambda b: (b, 0, _col(P_GB))),
            pl.BlockSpec((length, GROUP_W), lambda b: (0, 0)),
            pl.BlockSpec((GROUP_W, GROUP_W), lambda b: (0, 0)),
            pl.BlockSpec((1, GROUP_W), lambda b: (0, 0)),
        ],
        out_specs=pl.BlockSpec((1, length, GROUP_W), lambda b: (b, 0, 0)),
        out_shape=jax.ShapeDtypeStruct((bsz, length, GROUP_W), BF16),
        scratch_shapes=[pltpu.VMEM((length + 2 * POOL_HALO, GROUP_W), F32)],
        compiler_params=pltpu.CompilerParams(vmem_limit_bytes=VMEM_LIMIT),
        name="pool_mixer",
    )(p, p, cnt, w_bd, scale)


def _pool_counts(length):
    t = np.arange(length)
    cols = []
    for w in POOL_WINDOWS:
        lo = np.maximum(t - w // 2, 0)
        hi = np.minimum(t + w // 2 - 1, length - 1)
        cols.append(np.repeat((hi - lo + 1).astype(np.float32)[:, None], HEAD_DIM, axis=1))
    return np.concatenate(cols, axis=1)


def _out_kernel(x_ref, gate_ref, a_ref, b_ref, n_ref, w_ref, o_ref):
    y = (_dot(a_ref[0], w_ref[0:512, :]) + _dot(b_ref[0], w_ref[512:768, :])
         + _dot(n_ref[0], w_ref[768:1024, :]))
    o_ref[0] = x_ref[0] + gate_ref[0] * y


def _output(x, mods, mod_row, a, b, n, w_out, tm):
    bsz, length, _ = x.shape
    tok = lambda width: pl.BlockSpec((1, tm, width), lambda bb, i: (bb, i, 0))
    return pl.pallas_call(
        _out_kernel,
        grid=(bsz, length // tm),
        in_specs=[
            tok(D_MODEL),
            pl.BlockSpec((1, 1, D_MODEL), lambda bb, i: (mod_row(bb), 0, 2)),
            tok(2 * GROUP_W), tok(GROUP_W), tok(GROUP_W),
            pl.BlockSpec((D_MODEL, D_MODEL), lambda bb, i: (0, 0)),
        ],
        out_specs=tok(D_MODEL),
        out_shape=jax.ShapeDtypeStruct(x.shape, F32),
        compiler_params=pltpu.CompilerParams(vmem_limit_bytes=VMEM_LIMIT),
        name="output_projection",
    )(x, mods, a, b, n, w_out)


def _rope_tables(seq):
    t = jnp.arange(seq)
    half = HEAD_DIM // 4
    inv_freq = ROPE_THETA ** (-jnp.arange(half, dtype=jnp.float32) / half)
    cos, sa, sb = [], [], []
    zero = jnp.zeros((seq, half), F32)
    for pos in (t // GRID_W, t % GRID_W):
        ang = pos.astype(jnp.float32)[:, None] * inv_freq[None, :]
        cos += [jnp.cos(ang), jnp.cos(ang)]
        sa += [-jnp.sin(ang), zero]
        sb += [zero, jnp.sin(ang)]
    tile = lambda parts: jnp.tile(jnp.concatenate(parts, axis=-1), (1, LANES // HEAD_DIM))
    return tile(cos), tile(sa), tile(sb)


def _layout_constants():
    lane = np.arange(GROUP_W)
    seg = lane[:, None] // HEAD_DIM == lane[None, :] // HEAD_DIM
    src = np.arange(LANES)[:, None]
    col = np.arange(2 * GROUP_W)[None, :]
    rep_k = src == (col // GROUP_W) * HEAD_DIM + col % HEAD_DIM
    colv = np.arange(2 * LANES)[None, :]
    is_v = colv % LANES < HEAD_DIM
    rep_v = is_v & (src == (colv // LANES) * HEAD_DIM + colv % LANES)
    return (jnp.asarray(seg, BF16), jnp.asarray(rep_k, BF16), jnp.asarray(rep_v, BF16),
            jnp.asarray(~is_v, F32))


def kernel(x, c, ctx, c_ctx, norm_gain, w_mod, b_mod, w_in, att_q_gain, att_k_gain,
           pool_w, pool_scale, na_q_gain, na_k_gain, na_rpb, w_out):
    bsz, seq, _ = x.shape
    n_ctx = ctx.shape[1]
    assert seq == NA_GROUPS * NA_TQ and n_ctx % KV_CHUNK == 0 and bsz < MOD_ROWS
    assert bsz % CTX_NB == 0 and bsz % NA_NB == 0

    mods_all = _modulation(c, c_ctx, w_mod, b_mod)
    tables = _rope_tables(seq)
    tables_ctx = (jnp.ones((n_ctx, LANES), F32), jnp.zeros((n_ctx, LANES), F32),
                  jnp.zeros((n_ctx, LANES), F32))
    consts = _layout_constants()
    cnt, cnt_ctx = jnp.asarray(_pool_counts(seq)), jnp.asarray(_pool_counts(n_ctx))
    slot, col_ok = _na_bias_layout()
    w_in_b = w_in.astype(BF16)
    w_out_b = w_out.astype(BF16)
    wide = lambda g: jnp.tile(g, GROUP_W // HEAD_DIM)[None, :]
    lat_row = lambda b: b
    ctx_row = lambda b: bsz

    for l in range(DEPTH):
        last = l == DEPTH - 1
        mods = mods_all[l].reshape(MOD_ROWS, 1, 3 * D_MODEL)
        gains = (wide(att_q_gain[l]), wide(att_k_gain[l]), wide(na_q_gain[l]),
                 wide(na_k_gain[l]))
        ng = norm_gain[l][None, :]
        p = _projection(x, mods, lat_row, ng, w_in_b[l], tables, gains, consts, TOKEN_TILE)
        p_ctx = _projection(ctx, mods, ctx_row, ng, w_in_b[l], tables_ctx, gains, consts, 256)
        bias = _na_bias(na_rpb[l], slot, col_ok)
        w_bd = jax.scipy.linalg.block_diag(*[pool_w[l, g] for g in range(4)]).astype(BF16)
        ps = pool_scale[l][None, :]

        a = _gqa_latent(p, p_ctx)
        n = _na_latent(p, p_ctx, bias)
        bo = _pool(p, cnt, w_bd, ps)
        x_new = _output(x, mods, lat_row, a, bo, n, w_out_b[l], TOKEN_TILE)

        if not last:
            a_c = _gqa_context(p_ctx)
            n_c = _na_context(p_ctx)
            bo_c = _pool(p_ctx, cnt_ctx, w_bd, ps)
            ctx = _output(ctx, mods, ctx_row, a_c, bo_c, n_c, w_out_b[l], 256)
        x = x_new
    return x
```

```python
import functools
import math

import numpy as np
import jax
import jax.numpy as jnp
from jax import lax
from jax.experimental import pallas as pl
from jax.experimental.pallas import tpu as pltpu

F32 = jnp.float32
BF16 = jnp.bfloat16

D_MODEL = 1024
DEPTH = 4
GRID_W = 64
HEAD_DIM = 64
ROPE_THETA = 10000.0
EPS = 1e-6
ATTN_SCALE = HEAD_DIM ** -0.5
LOG2E = 1.4426950408889634
Q_SCALE = ATTN_SCALE * LOG2E
POOL_WINDOWS = (2, 4, 8, 16)
NA_KH = 8
NA_KW = 16
IN_WIDTH = 2816

LANES = 128
GROUP_W = 256
HEADS_PER_GROUP = GROUP_W // HEAD_DIM
KV_CHUNK = 256
TILES_PER_ITER = 16

W_AQ, W_AK, W_AV, W_AG, W_BZ, W_BG, W_NQ, W_NK, W_NV, W_NG = (
    0, 512, 640, 768, 1280, 1536, 1792, 2048, 2304, 2560)
P_QA, P_KE, P_VE, P_GA, P_BZ, P_GB, P_NQ, P_NK, P_NV, P_GN = (
    0, 512, 1024, 1280, 1792, 2048, 2304, 2560, 2816, 3072)
P_WIDTH = 3328
MOD_ROWS = 24
TOKEN_TILE = 512
NEG = -1e30
VMEM_LIMIT = 48 * 1024 * 1024


def _silu(v):
    return v / (1.0 + jnp.exp(-v))


def _split_bf16(v):
    hi = v.astype(BF16)
    lo = (v - hi.astype(F32)).astype(BF16)
    return hi, lo


def _dot(a, b):
    return jnp.dot(a, b, preferred_element_type=F32)


def _mod_kernel(c_ref, w_ref, b_ref, o_ref):
    a_hi, a_lo = _split_bf16(_silu(c_ref[...]))
    w_hi, w_lo = _split_bf16(w_ref[0])
    o_ref[0] = _dot(a_hi, w_hi) + _dot(a_lo, w_hi) + _dot(a_hi, w_lo) + b_ref[0]


def _modulation(c, c_ctx, w_mod, b_mod):
    bsz = c.shape[0]
    cs = jnp.concatenate(
        [c, c_ctx[None], jnp.zeros((MOD_ROWS - bsz - 1, D_MODEL), F32)], axis=0)
    return pl.pallas_call(
        _mod_kernel,
        grid=(DEPTH, 3),
        in_specs=[
            pl.BlockSpec((MOD_ROWS, D_MODEL), lambda l, j: (0, 0)),
            pl.BlockSpec((1, D_MODEL, D_MODEL), lambda l, j: (l, 0, j)),
            pl.BlockSpec((1, 1, D_MODEL), lambda l, j: (l, 0, j)),
        ],
        out_specs=pl.BlockSpec((1, MOD_ROWS, D_MODEL), lambda l, j: (l, 0, j)),
        out_shape=jax.ShapeDtypeStruct((DEPTH, MOD_ROWS, 3 * D_MODEL), F32),
        compiler_params=pltpu.CompilerParams(vmem_limit_bytes=VMEM_LIMIT),
        name="modulation",
    )(cs, w_mod, b_mod.reshape(DEPTH, 1, 3 * D_MODEL))


def _proj_kernel(x_ref, shift_ref, scale_ref, ng_ref, w_ref, cos_ref, sa_ref, sb_ref,
                 aqg_ref, akg_ref, nqg_ref, nkg_ref, seg_ref, repk_ref, repv_ref, ones_ref,
                 o_ref, hb0_ref, hb1_ref, y0_ref, y1_ref):
    step = pl.program_id(0)

    @pl.when(step == 0)
    def _():
        hb1_ref[...] = jnp.zeros_like(hb1_ref)
        y0_ref[...] = jnp.zeros_like(y0_ref)
        y1_ref[...] = jnp.zeros_like(y1_ref)

    def pre_norm(hb_ref):
        x = x_ref[0]
        ms = jnp.mean(x * x, axis=-1, keepdims=True)
        h = x * lax.rsqrt(ms + EPS) * ng_ref[...]
        h = h * (1.0 + scale_ref[0]) + shift_ref[0]
        hb_ref[...] = h.astype(BF16)

    def stage(hb_w, hb_r, y_w, y_r):
        pre_norm(hb_w)
        epilogue = _proj_epilogue(y_r, cos_ref, sa_ref, sb_ref, aqg_ref, akg_ref, nqg_ref,
                                  nkg_ref, seg_ref, repk_ref, repv_ref, ones_ref, o_ref)
        for c0 in range(0, IN_WIDTH, GROUP_W):
            y_w[:, c0:c0 + GROUP_W] = _dot(hb_r[...], w_ref[:, c0:c0 + GROUP_W])
            next(epilogue)

    @pl.when(step % 2 == 0)
    def _():
        stage(hb0_ref, hb1_ref, y1_ref, y0_ref)

    @pl.when(step % 2 == 1)
    def _():
        stage(hb1_ref, hb0_ref, y0_ref, y1_ref)


def _proj_epilogue(y_ref, cos_ref, sa_ref, sb_ref, aqg_ref, akg_ref, nqg_ref, nkg_ref,
                   seg_ref, repk_ref, repv_ref, ones_ref, o_ref):
    def proj(c0, width):
        return y_ref[:, c0:c0 + width]

    def head_norm(y, g):
        ss = _dot((y * y).astype(BF16), seg_ref[...])
        return y * lax.rsqrt(ss * (1.0 / HEAD_DIM) + EPS) * g

    def rope(y):
        width = y.shape[-1]
        wide = lambda t: jnp.tile(t[...], (1, width // LANES))
        return (y * wide(cos_ref) + pltpu.roll(y, width - 16, 1) * wide(sa_ref)
                + pltpu.roll(y, 16, 1) * wide(sb_ref))

    def put(c0, v):
        o_ref[0, :, c0:c0 + v.shape[-1]] = v.astype(BF16)

    for j in range(2):
        y = proj(W_AQ + j * GROUP_W, GROUP_W)
        put(P_QA + j * GROUP_W, rope(head_norm(y, aqg_ref[...])) * Q_SCALE)
        yield
    kv = proj(W_AK, GROUP_W)
    kb = rope(head_norm(kv, akg_ref[...]))[:, :LANES].astype(BF16)
    put(P_KE, _dot(kb, repk_ref[...]))
    put(P_VE, _dot(kv[:, LANES:].astype(BF16), repv_ref[...]) + ones_ref[...])
    yield
    for j in range(2):
        put(P_GA + j * GROUP_W, _silu(proj(W_AG + j * GROUP_W, GROUP_W)))
        yield
    put(P_BZ, proj(W_BZ, 256))
    yield
    put(P_GB, _silu(proj(W_BG, 256)))
    yield
    put(P_NQ, head_norm(proj(W_NQ, GROUP_W), nqg_ref[...]) * Q_SCALE)
    yield
    put(P_NK, head_norm(proj(W_NK, GROUP_W), nkg_ref[...]))
    yield
    put(P_NV, proj(W_NV, 256))
    yield
    put(P_GN, _silu(proj(W_NG, 256)))
    yield


def _projection(x, mods, mod_row, norm_gain, w_in, tables, gains, consts, tm):
    bsz, length, _ = x.shape
    cos, sa, sb = tables
    per_seq = length // tm
    n_tiles = bsz * per_seq
    norm_tile = lambda i: jnp.minimum(i, n_tiles - 1)
    out_tile = lambda i: jnp.maximum(i - 2, 0)
    vec = lambda k: pl.BlockSpec(
        (1, 1, D_MODEL), lambda i: (mod_row(norm_tile(i) // per_seq), 0, k))
    const2 = lambda a: pl.BlockSpec(a.shape, lambda i: (0, 0))
    tab = pl.BlockSpec((tm, LANES), lambda i: (out_tile(i) % per_seq, 0))
    hidden = pltpu.VMEM((tm, D_MODEL), BF16)
    projected = pltpu.VMEM((tm, IN_WIDTH), F32)
    return pl.pallas_call(
        _proj_kernel,
        grid=(n_tiles + 2,),
        in_specs=[
            pl.BlockSpec((1, tm, D_MODEL),
                         lambda i: (norm_tile(i) // per_seq, norm_tile(i) % per_seq, 0)),
            vec(0), vec(1), const2(norm_gain), const2(w_in), tab, tab, tab,
            *[const2(g) for g in gains], *[const2(a) for a in consts],
        ],
        out_specs=pl.BlockSpec((1, tm, P_WIDTH),
                               lambda i: (out_tile(i) // per_seq, out_tile(i) % per_seq, 0)),
        out_shape=jax.ShapeDtypeStruct((bsz, length, P_WIDTH), BF16),
        scratch_shapes=[hidden, hidden, projected, projected],
        compiler_params=pltpu.CompilerParams(
            vmem_limit_bytes=VMEM_LIMIT, dimension_semantics=("arbitrary",)),
        name="projection",
    )(x, mods, mods, norm_gain, w_in, cos, sa, sb, *gains, *consts)


def _attn_kernel(*refs, tq, tiles_per_row, n_tiles, tiles_per_iter, n_kv, chunks, n_bias,
                 sums_in_v):
    q_ref, g_ref = refs[0], refs[1]
    kv = refs[2:2 + 2 * n_kv]
    pos = 2 + 2 * n_kv
    bias_ref = refs[pos] if n_bias else None
    pos += 1 if n_bias else 0
    o_ref, s_a, s_b, m_a, m_b, mc_ref, lc_ref = refs[pos:pos + 7]
    rows = HEADS_PER_GROUP * tq
    n_chunks = len(chunks)
    lane_head = jnp.right_shift(lax.broadcasted_iota(jnp.int32, (tq, GROUP_W), 1), 6)
    low_half = lax.broadcasted_iota(jnp.int32, (tq, LANES), 1) < HEAD_DIM
    shift = tiles_per_row.bit_length() - 1

    def tile_pos(t):
        if tiles_per_row == 1:
            return t, 0
        r0 = (t & (tiles_per_row - 1)) * tq
        return t >> shift, (r0 if isinstance(r0, int) else pl.multiple_of(r0, tq))

    def stage(t1, s1, m1, t0, s0, m0):
        if t1 is not None:
            b1, r1 = tile_pos(t1)
            q = q_ref[b1, pl.ds(r1, tq), :].astype(F32)
            qs = jnp.concatenate([jnp.where(lane_head == h, q, 0.0).astype(BF16)
                                  for h in range(HEADS_PER_GROUP)], axis=0)
        if t0 is not None:
            b0, r0 = tile_pos(t0)
            m_prev = m0[...]
            m_prev = jnp.concatenate([m_prev, m_prev], axis=-1)
            acc = None
        for ci, (ai, off) in enumerate(chunks):
            if t1 is not None:
                k = kv[2 * ai][b1, off:off + KV_CHUNK, :]
                s = lax.dot_general(qs, k, (((1,), (1,)), ((), ())),
                                    preferred_element_type=F32)
                if ci < n_bias:
                    s = s + bias_ref[0, :, ci * KV_CHUNK:(ci + 1) * KV_CHUNK]
                s1[ci] = s
                mc_ref[ci] = jnp.maximum(s[:, :LANES], s[:, LANES:])
            if t0 is not None:
                p = jnp.exp2(s0[ci] - m_prev)
                if not sums_in_v:
                    lc_ref[ci] = p[:, :LANES] + p[:, LANES:]
                d = _dot(p.astype(BF16), kv[2 * ai + 1][b0, off:off + KV_CHUNK, :])
                acc = d if acc is None else acc + d
        if t1 is not None:
            m = mc_ref[0]
            for ci in range(1, n_chunks):
                m = jnp.maximum(m, mc_ref[ci])
            m1[...] = jnp.broadcast_to(jnp.max(m, axis=-1, keepdims=True), (rows, LANES))
        if t0 is not None:
            gate = g_ref[b0, pl.ds(r0, tq), :].astype(F32)
            head = lambda a, h: a[h * tq:(h + 1) * tq, :]
            if sums_in_v:
                o = acc / pltpu.roll(acc, HEAD_DIM, 1)
                halves = [jnp.where(low_half, head(o, h), pltpu.roll(head(o, h + 1), HEAD_DIM, 1))
                          for h in (0, 2)]
                out = jnp.concatenate(halves, axis=-1)
            else:
                l = lc_ref[0]
                for ci in range(1, n_chunks):
                    l = l + lc_ref[ci]
                o = acc / jnp.sum(l, axis=-1, keepdims=True)
                out = jnp.zeros((tq, GROUP_W), F32)
                for h in range(HEADS_PER_GROUP):
                    out = out + jnp.where(lane_head == h, head(o, h), 0.0)
            o_ref[b0, pl.ds(r0, tq), :] = (out * gate).astype(BF16)

    stage(0, s_a, m_a, None, None, None)

    if tiles_per_iter == n_tiles:
        for t in range(0, n_tiles, 2):
            stage(t + 1, s_b, m_b, t, s_a, m_a)
            stage(t + 2 if t + 2 < n_tiles else None, s_a, m_a, t + 1, s_b, m_b)
        return

    def body(j, carry):
        t = tiles_per_iter * j
        for u in range(0, tiles_per_iter, 2):
            stage(t + u + 1, s_b, m_b, t + u, s_a, m_a)
            stage(jnp.minimum(t + u + 2, n_tiles - 1), s_a, m_a, t + u + 1, s_b, m_b)
        return carry

    lax.fori_loop(0, n_tiles // tiles_per_iter, body, 0)


def _attention(grid, nb, q_rows, q, gate, kvs, bias, out_shape, out_map, tq, chunks, name):
    n_kv = len(kvs) // 2
    n_bias = 0
    operands = [q[0], gate[0]]
    specs = [pl.BlockSpec((nb, q_rows, GROUP_W), q[1]),
             pl.BlockSpec((nb, q_rows, GROUP_W), gate[1])]
    for arr, kv_rows, kv_lanes, imap in kvs:
        operands.append(arr)
        specs.append(pl.BlockSpec((nb, kv_rows, kv_lanes), imap))
    sums_in_v = kvs[1][2] == LANES
    if bias is not None:
        arr, imap = bias
        n_bias = arr.shape[-1] // KV_CHUNK
        operands.append(arr)
        specs.append(pl.BlockSpec((1,) + arr.shape[1:], imap))
    rows = HEADS_PER_GROUP * tq
    tiles_per_row = q_rows // tq
    n_tiles = nb * tiles_per_row
    assert n_tiles % 2 == 0 and tiles_per_row & (tiles_per_row - 1) == 0
    score_buf = pltpu.VMEM((len(chunks), rows, KV_CHUNK), F32)
    max_buf = pltpu.VMEM((rows, LANES), F32)
    part_buf = pltpu.VMEM((len(chunks), rows, LANES), F32)
    sum_buf = pltpu.VMEM((1, 8, LANES) if sums_in_v else (len(chunks), rows, LANES), F32)
    return pl.pallas_call(
        functools.partial(_attn_kernel, tq=tq, tiles_per_row=tiles_per_row, n_tiles=n_tiles,
                          tiles_per_iter=math.gcd(n_tiles, TILES_PER_ITER), n_kv=n_kv,
                          chunks=chunks, n_bias=n_bias, sums_in_v=sums_in_v),
        grid=grid,
        in_specs=specs,
        out_specs=pl.BlockSpec((nb, q_rows, GROUP_W), out_map),
        out_shape=jax.ShapeDtypeStruct(out_shape, BF16),
        scratch_shapes=[score_buf, score_buf, max_buf, max_buf, part_buf, sum_buf],
        compiler_params=pltpu.CompilerParams(vmem_limit_bytes=VMEM_LIMIT),
        name=name,
    )(*operands)


def _col(off):
    return off // GROUP_W


GQA_TQ = 128
CTX_NB = 8


def _gqa_kv(p):
    rows = p.shape[1]
    return [(p, rows, GROUP_W, lambda b, j: (b, 0, _col(P_KE) + j)),
            (p, rows, LANES, lambda b, j: (b, 0, P_VE // LANES + j))]


def _gqa_latent(p, p_ctx):
    bsz, length, _ = p.shape
    n_ctx = p_ctx.shape[1]
    kvs = _gqa_kv(p) + _gqa_kv(p_ctx)
    chunks = tuple((0, c * KV_CHUNK) for c in range(length // KV_CHUNK)) + tuple(
        (1, c * KV_CHUNK) for c in range(n_ctx // KV_CHUNK))
    return _attention(
        grid=(bsz, 2), nb=1, q_rows=length,
        q=(p, lambda b, j: (b, 0, _col(P_QA) + j)),
        gate=(p, lambda b, j: (b, 0, _col(P_GA) + j)),
        kvs=kvs, bias=None,
        out_shape=(bsz, length, 2 * GROUP_W), out_map=lambda b, j: (b, 0, j),
        tq=GQA_TQ, chunks=chunks, name="gqa_attention")


def _gqa_context(p_ctx):
    bsz, n_ctx, _ = p_ctx.shape
    return _attention(
        grid=(bsz // CTX_NB, 2), nb=CTX_NB, q_rows=n_ctx,
        q=(p_ctx, lambda b, j: (b, 0, _col(P_QA) + j)),
        gate=(p_ctx, lambda b, j: (b, 0, _col(P_GA) + j)),
        kvs=_gqa_kv(p_ctx), bias=None,
        out_shape=(bsz, n_ctx, 2 * GROUP_W), out_map=lambda b, j: (b, 0, j),
        tq=GQA_TQ, chunks=tuple((0, c * KV_CHUNK) for c in range(n_ctx // KV_CHUNK)),
        name="gqa_context_attention")


NA_TQ = 256
NA_BAND = 3
NA_GROUPS = 2048 // NA_TQ
NA_NB = 4
NA_PATTERN_GROUPS = (0, 1, NA_GROUPS - 1)


def _na_band_start(g):
    return jnp.clip(g - 1, 0, NA_GROUPS - NA_BAND)


def _na_latent(p, p_ctx, bias):
    bsz, length, _ = p.shape
    kvs = []
    for i in range(NA_BAND):
        kvs.append((p, KV_CHUNK, GROUP_W,
                    lambda g, b, i=i: (b, _na_band_start(g) + i, _col(P_NK))))
        kvs.append((p, KV_CHUNK, GROUP_W,
                    lambda g, b, i=i: (b, _na_band_start(g) + i, _col(P_NV))))
    n_ctx = p_ctx.shape[1]
    kvs += [(p_ctx, n_ctx, GROUP_W, lambda g, b: (b, 0, _col(P_NK))),
            (p_ctx, n_ctx, GROUP_W, lambda g, b: (b, 0, _col(P_NV)))]
    chunks = tuple((i, 0) for i in range(NA_BAND)) + tuple(
        (NA_BAND, c * KV_CHUNK) for c in range(n_ctx // KV_CHUNK))
    pattern = lambda g, b: (jnp.where(g == 0, 0, jnp.where(g == NA_GROUPS - 1, 2, 1)), 0, 0)
    return _attention(
        grid=(NA_GROUPS, bsz // NA_NB), nb=NA_NB, q_rows=NA_TQ,
        q=(p, lambda g, b: (b, g, _col(P_NQ))),
        gate=(p, lambda g, b: (b, g, _col(P_GN))),
        kvs=kvs, bias=(bias, pattern),
        out_shape=(bsz, length, GROUP_W), out_map=lambda g, b: (b, g, 0),
        tq=NA_TQ, chunks=chunks, name="na_attention")


def _na_context(p_ctx):
    bsz, n_ctx, _ = p_ctx.shape
    kvs = [(p_ctx, n_ctx, GROUP_W, lambda b: (b, 0, _col(P_NK))),
           (p_ctx, n_ctx, GROUP_W, lambda b: (b, 0, _col(P_NV)))]
    return _attention(
        grid=(bsz // CTX_NB,), nb=CTX_NB, q_rows=n_ctx,
        q=(p_ctx, lambda b: (b, 0, _col(P_NQ))),
        gate=(p_ctx, lambda b: (b, 0, _col(P_GN))),
        kvs=kvs, bias=None,
        out_shape=(bsz, n_ctx, GROUP_W), out_map=lambda b: (b, 0, 0),
        tq=NA_TQ, chunks=tuple((0, c * KV_CHUNK) for c in range(n_ctx // KV_CHUNK)),
        name="na_context_attention")


def _na_bias_layout():
    rows = 2048 // GRID_W
    rows_per_tile, rows_per_chunk = NA_TQ // GRID_W, KV_CHUNK // GRID_W
    band_rows = NA_BAND * rows_per_chunk
    slot = np.full((len(NA_PATTERN_GROUPS), rows_per_tile, band_rows), 2 * NA_KH - 1)
    for pat, g in enumerate(NA_PATTERN_GROUPS):
        band0 = rows_per_chunk * min(max(g - 1, 0), NA_GROUPS - NA_BAND)
        for qr in range(rows_per_tile):
            r = rows_per_tile * g + qr
            r0 = min(max(r - NA_KH // 2, 0), rows - NA_KH)
            for kr in range(band_rows):
                rp = band0 + kr
                if r0 <= rp < r0 + NA_KH:
                    slot[pat, qr, kr] = rp - r + NA_KH - 1
    c = np.arange(GRID_W)
    c0 = np.clip(c - NA_KW // 2, 0, GRID_W - NA_KW)
    col_ok = (c[None, :] >= c0[:, None]) & (c[None, :] < c0[:, None] + NA_KW)
    return slot, col_ok


def _na_bias(rpb, slot, col_ok):
    heads, n_dr, _ = rpb.shape
    rpb = rpb * LOG2E
    w = jnp.concatenate([rpb[..., NA_KW - 1:],
                         jnp.full((heads, n_dr, LANES - (2 * NA_KW - 1)), NEG, F32),
                         rpb[..., :NA_KW - 1]], axis=-1)
    t = jnp.tile(w, (1, 1, GRID_W))[..., :GRID_W * (LANES - 1)]
    t = t.reshape(heads, n_dr, GRID_W, LANES - 1)[..., :GRID_W]
    t = jnp.where(col_ok[None, None], t, NEG)
    t = jnp.concatenate([t, jnp.full((heads, 1, GRID_W, GRID_W), NEG, F32)], axis=1)
    n_pat, n_qr, n_kr = slot.shape
    blocks = jnp.stack([t[:, int(s)] for s in slot.reshape(-1)], axis=1)
    blocks = blocks.reshape(heads, n_pat, n_qr, n_kr, GRID_W, GRID_W)
    blocks = jnp.transpose(blocks, (1, 0, 2, 4, 3, 5))
    return blocks.reshape(n_pat, heads * n_qr * GRID_W, n_kr * GRID_W)


POOL_HALO = 8
POOL_ROWS = 256


def _pool_kernel(z_ref, g_ref, cnt_ref, w_ref, s_ref, o_ref, zp_ref, *, length):
    zeros = jnp.zeros((POOL_HALO, GROUP_W), F32)
    zp_ref[0:POOL_HALO, :] = zeros
    zp_ref[POOL_HALO + length:2 * POOL_HALO + length, :] = zeros
    zp_ref[POOL_HALO:POOL_HALO + length, :] = z_ref[0].astype(F32)
    first = lax.broadcasted_iota(jnp.int32, (POOL_ROWS, LANES), 1) < HEAD_DIM

    def window(t0, lanes, offsets):
        acc = None
        for o in offsets:
            v = zp_ref[POOL_HALO + t0 + o:POOL_HALO + t0 + o + POOL_ROWS, lanes]
            acc = v if acc is None else acc + v
        return acc

    for t0 in range(0, length, POOL_ROWS):
        lo, hi = slice(0, LANES), slice(LANES, 2 * LANES)
        s2 = window(t0, lo, (-1, 0))
        s4 = s2 + window(t0, lo, (-2, 1))
        s8 = window(t0, hi, range(-4, 4))
        s16 = s8 + window(t0, hi, tuple(range(-8, -4)) + tuple(range(4, 8)))
        win = jnp.concatenate([jnp.where(first, s2, s4), jnp.where(first, s8, s16)], axis=-1)
        rows = slice(t0, t0 + POOL_ROWS)
        z = zp_ref[POOL_HALO + t0:POOL_HALO + t0 + POOL_ROWS, :]
        pooled = win / cnt_ref[rows, :] - z
        y = _dot(pooled.astype(BF16), w_ref[...]) * s_ref[...]
        o_ref[0, rows, :] = (y * g_ref[0, rows, :].astype(F32)).astype(BF16)


def _pool(p, cnt, w_bd, scale):
    bsz, length, _ = p.shape
    return pl.pallas_call(
        functools.partial(_pool_kernel, length=length),
        grid=(bsz,),
        in_specs=[
            pl.BlockSpec((1, length, GROUP_W), lambda b: (b, 0, _col(P_BZ))),
            pl.BlockSpec((1, length, GROUP_W), lambda b: (b, 0, _col(P_GB))),
            pl.BlockSpec((length, GROUP_W), lambda b: (0, 0)),
            pl.BlockSpec((GROUP_W, GROUP_W), lambda b: (0, 0)),
            pl.BlockSpec((1, GROUP_W), lambda b: (0, 0)),
        ],
        out_specs=pl.BlockSpec((1, length, GROUP_W), lambda b: (b, 0, 0)),
        out_shape=jax.ShapeDtypeStruct((bsz, length, GROUP_W), BF16),
        scratch_shapes=[pltpu.VMEM((length + 2 * POOL_HALO, GROUP_W), F32)],
        compiler_params=pltpu.CompilerParams(vmem_limit_bytes=VMEM_LIMIT),
        name="pool_mixer",
    )(p, p, cnt, w_bd, scale)


def _pool_counts(length):
    t = np.arange(length)
    cols = []
    for w in POOL_WINDOWS:
        lo = np.maximum(t - w // 2, 0)
        hi = np.minimum(t + w // 2 - 1, length - 1)
        cols.append(np.repeat((hi - lo + 1).astype(np.float32)[:, None], HEAD_DIM, axis=1))
    return np.concatenate(cols, axis=1)


def _out_kernel(x_ref, gate_ref, a_ref, b_ref, n_ref, w_ref, o_ref):
    y = (_dot(a_ref[0], w_ref[0:512, :]) + _dot(b_ref[0], w_ref[512:768, :])
         + _dot(n_ref[0], w_ref[768:1024, :]))
    o_ref[0] = x_ref[0] + gate_ref[0] * y


def _output(x, mods, mod_row, a, b, n, w_out, tm):
    bsz, length, _ = x.shape
    tok = lambda width: pl.BlockSpec((1, tm, width), lambda bb, i: (bb, i, 0))
    return pl.pallas_call(
        _out_kernel,
        grid=(bsz, length // tm),
        in_specs=[
            tok(D_MODEL),
            pl.BlockSpec((1, 1, D_MODEL), lambda bb, i: (mod_row(bb), 0, 2)),
            tok(2 * GROUP_W), tok(GROUP_W), tok(GROUP_W),
            pl.BlockSpec((D_MODEL, D_MODEL), lambda bb, i: (0, 0)),
        ],
        out_specs=tok(D_MODEL),
        out_shape=jax.ShapeDtypeStruct(x.shape, F32),
        compiler_params=pltpu.CompilerParams(vmem_limit_bytes=VMEM_LIMIT),
        name="output_projection",
    )(x, mods, a, b, n, w_out)


def _rope_tables(seq):
    t = jnp.arange(seq)
    half = HEAD_DIM // 4
    inv_freq = ROPE_THETA ** (-jnp.arange(half, dtype=jnp.float32) / half)
    cos, sa, sb = [], [], []
    zero = jnp.zeros((seq, half), F32)
    for pos in (t // GRID_W, t % GRID_W):
        ang = pos.astype(jnp.float32)[:, None] * inv_freq[None, :]
        cos += [jnp.cos(ang), jnp.cos(ang)]
        sa += [-jnp.sin(ang), zero]
        sb += [zero, jnp.sin(ang)]
    tile = lambda parts: jnp.tile(jnp.concatenate(parts, axis=-1), (1, LANES // HEAD_DIM))
    return tile(cos), tile(sa), tile(sb)


def _layout_constants():
    lane = np.arange(GROUP_W)
    seg = lane[:, None] // HEAD_DIM == lane[None, :] // HEAD_DIM
    src = np.arange(LANES)[:, None]
    col = np.arange(2 * GROUP_W)[None, :]
    rep_k = src == (col // GROUP_W) * HEAD_DIM + col % HEAD_DIM
    colv = np.arange(2 * LANES)[None, :]
    is_v = colv % LANES < HEAD_DIM
    rep_v = is_v & (src == (colv // LANES) * HEAD_DIM + colv % LANES)
    return (jnp.asarray(seg, BF16), jnp.asarray(rep_k, BF16), jnp.asarray(rep_v, BF16),
            jnp.asarray(~is_v, F32))


def kernel(x, c, ctx, c_ctx, norm_gain, w_mod, b_mod, w_in, att_q_gain, att_k_gain,
           pool_w, pool_scale, na_q_gain, na_k_gain, na_rpb, w_out):
    bsz, seq, _ = x.shape
    n_ctx = ctx.shape[1]
    assert seq == NA_GROUPS * NA_TQ and n_ctx % KV_CHUNK == 0 and bsz < MOD_ROWS
    assert bsz % CTX_NB == 0 and bsz % NA_NB == 0

    mods_all = _modulation(c, c_ctx, w_mod, b_mod)
    tables = _rope_tables(seq)
    tables_ctx = (jnp.ones((n_ctx, LANES), F32), jnp.zeros((n_ctx, LANES), F32),
                  jnp.zeros((n_ctx, LANES), F32))
    consts = _layout_constants()
    cnt, cnt_ctx = jnp.asarray(_pool_counts(seq)), jnp.asarray(_pool_counts(n_ctx))
    slot, col_ok = _na_bias_layout()
    w_in_b = w_in.astype(BF16)
    w_out_b = w_out.astype(BF16)
    wide = lambda g: jnp.tile(g, GROUP_W // HEAD_DIM)[None, :]
    lat_row = lambda b: b
    ctx_row = lambda b: bsz

    for l in range(DEPTH):
        last = l == DEPTH - 1
        mods = mods_all[l].reshape(MOD_ROWS, 1, 3 * D_MODEL)
        gains = (wide(att_q_gain[l]), wide(att_k_gain[l]), wide(na_q_gain[l]),
                 wide(na_k_gain[l]))
        ng = norm_gain[l][None, :]
        p = _projection(x, mods, lat_row, ng, w_in_b[l], tables, gains, consts, TOKEN_TILE)
        p_ctx = _projection(ctx, mods, ctx_row, ng, w_in_b[l], tables_ctx, gains, consts, 256)
        bias = _na_bias(na_rpb[l], slot, col_ok)
        w_bd = jax.scipy.linalg.block_diag(*[pool_w[l, g] for g in range(4)]).astype(BF16)
        ps = pool_scale[l][None, :]

        a = _gqa_latent(p, p_ctx)
        n = _na_latent(p, p_ctx, bias)
        bo = _pool(p, cnt, w_bd, ps)
        x_new = _output(x, mods, lat_row, a, bo, n, w_out_b[l], TOKEN_TILE)

        if not last:
            a_c = _gqa_context(p_ctx)
            n_c = _na_context(p_ctx)
            bo_c = _pool(p_ctx, cnt_ctx, w_bd, ps)
            ctx = _output(ctx, mods, ctx_row, a_c, bo_c, n_c, w_out_b[l], 256)
        x = x_new
    return x
```

```python
import functools
import math

import numpy as np
import jax
import jax.numpy as jnp
from jax import lax
from jax.experimental import pallas as pl
from jax.experimental.pallas import tpu as pltpu

F32 = jnp.float32
BF16 = jnp.bfloat16
F8 = jnp.float8_e4m3fn
P_SHIFT = 8.0
F8_MAX = 448.0
F8_TARGET = 256.0
TINY = 1e-30
FP8_GAIN_LIMIT = 2.0

D_MODEL = 1024
DEPTH = 4
GRID_W = 64
HEAD_DIM = 64
ROPE_THETA = 10000.0
EPS = 1e-6
ATTN_SCALE = HEAD_DIM ** -0.5
LOG2E = 1.4426950408889634
Q_SCALE = ATTN_SCALE * LOG2E
POOL_WINDOWS = (2, 4, 8, 16)
NA_KH = 8
NA_KW = 16
IN_WIDTH = 2816

LANES = 128
GROUP_W = 256
HEADS_PER_GROUP = GROUP_W // HEAD_DIM
KV_CHUNK = 256
TILES_PER_ITER = 16

W_AQ, W_AK, W_AV, W_AG, W_BZ, W_BG, W_NQ, W_NK, W_NV, W_NG = (
    0, 512, 640, 768, 1280, 1536, 1792, 2048, 2304, 2560)
P_QA, P_KE, P_VE, P_GA, P_BZ, P_GB, P_NQ, P_NK, P_NV, P_GN = (
    0, 512, 1024, 1280, 1792, 2048, 2304, 2560, 2816, 3072)
P_WIDTH = 3328
MOD_ROWS = 24
TOKEN_TILE = 512
OUT_TILE = 1024
NEG = -1e30
VMEM_LIMIT = 48 * 1024 * 1024


def _silu(v):
    return v / (1.0 + jnp.exp(-v))


def _split_bf16(v):
    hi = v.astype(BF16)
    lo = (v - hi.astype(F32)).astype(BF16)
    return hi, lo


def _dot(a, b):
    return jnp.dot(a, b, preferred_element_type=F32)


def _mod_kernel(c_ref, w_ref, b_ref, o_ref):
    a_hi, a_lo = _split_bf16(_silu(c_ref[...]))
    w_hi, w_lo = _split_bf16(w_ref[0])
    o_ref[0] = _dot(a_hi, w_hi) + _dot(a_lo, w_hi) + _dot(a_hi, w_lo) + b_ref[0]


def _modulation(c, c_ctx, w_mod, b_mod):
    bsz = c.shape[0]
    cs = jnp.concatenate(
        [c, c_ctx[None], jnp.zeros((MOD_ROWS - bsz - 1, D_MODEL), F32)], axis=0)
    return pl.pallas_call(
        _mod_kernel,
        grid=(DEPTH, 3),
        in_specs=[
            pl.BlockSpec((MOD_ROWS, D_MODEL), lambda l, j: (0, 0)),
            pl.BlockSpec((1, D_MODEL, D_MODEL), lambda l, j: (l, 0, j)),
            pl.BlockSpec((1, 1, D_MODEL), lambda l, j: (l, 0, j)),
        ],
        out_specs=pl.BlockSpec((1, MOD_ROWS, D_MODEL), lambda l, j: (l, 0, j)),
        out_shape=jax.ShapeDtypeStruct((DEPTH, MOD_ROWS, 3 * D_MODEL), F32),
        compiler_params=pltpu.CompilerParams(vmem_limit_bytes=VMEM_LIMIT),
        name="modulation",
    )(cs, w_mod, b_mod.reshape(DEPTH, 1, 3 * D_MODEL))


def _proj_kernel(x_ref, shift_ref, scale_ref, ng_ref, w_ref, cos_ref, sa_ref, sb_ref,
                 aqg_ref, akg_ref, nqg_ref, nkg_ref, seg_ref, repk_ref, repv_ref, ones_ref,
                 o_ref, hb0_ref, hb1_ref, y0_ref, y1_ref):
    step = pl.program_id(0)

    @pl.when(step == 0)
    def _():
        hb1_ref[...] = jnp.zeros_like(hb1_ref)
        y0_ref[...] = jnp.zeros_like(y0_ref)
        y1_ref[...] = jnp.zeros_like(y1_ref)

    def pre_norm(hb_ref):
        x = x_ref[0]
        ms = jnp.mean(x * x, axis=-1, keepdims=True)
        h = x * lax.rsqrt(ms + EPS) * ng_ref[...]
        h = h * (1.0 + scale_ref[0]) + shift_ref[0]
        hb_ref[...] = h.astype(BF16)

    def stage(hb_w, hb_r, y_w, y_r):
        pre_norm(hb_w)
        epilogue = _proj_epilogue(y_r, cos_ref, sa_ref, sb_ref, aqg_ref, akg_ref, nqg_ref,
                                  nkg_ref, seg_ref, repk_ref, repv_ref, ones_ref, o_ref)
        for c0 in range(0, IN_WIDTH, GROUP_W):
            y_w[:, c0:c0 + GROUP_W] = _dot(hb_r[...], w_ref[:, c0:c0 + GROUP_W])
            next(epilogue)

    @pl.when(step % 2 == 0)
    def _():
        stage(hb0_ref, hb1_ref, y1_ref, y0_ref)

    @pl.when(step % 2 == 1)
    def _():
        stage(hb1_ref, hb0_ref, y0_ref, y1_ref)


def _proj_epilogue(y_ref, cos_ref, sa_ref, sb_ref, aqg_ref, akg_ref, nqg_ref, nkg_ref,
                   seg_ref, repk_ref, repv_ref, ones_ref, o_ref):
    def proj(c0, width):
        return y_ref[:, c0:c0 + width]

    def head_norm(y, g):
        ss = _dot((y * y).astype(BF16), seg_ref[...])
        return y * lax.rsqrt(ss * (1.0 / HEAD_DIM) + EPS) * g

    def rope(y):
        width = y.shape[-1]
        wide = lambda t: jnp.tile(t[...], (1, width // LANES))
        return (y * wide(cos_ref) + pltpu.roll(y, width - 16, 1) * wide(sa_ref)
                + pltpu.roll(y, 16, 1) * wide(sb_ref))

    def put(c0, v):
        o_ref[0, :, c0:c0 + v.shape[-1]] = v.astype(BF16)

    for j in range(2):
        y = proj(W_AQ + j * GROUP_W, GROUP_W)
        put(P_QA + j * GROUP_W, rope(head_norm(y, aqg_ref[...])) * Q_SCALE)
        yield
    kv = proj(W_AK, GROUP_W)
    kb = rope(head_norm(kv, akg_ref[...]))[:, :LANES].astype(BF16)
    put(P_KE, _dot(kb, repk_ref[...]))
    put(P_VE, _dot(kv[:, LANES:].astype(BF16), repv_ref[...]) + ones_ref[...])
    yield
    for j in range(2):
        put(P_GA + j * GROUP_W, _silu(proj(W_AG + j * GROUP_W, GROUP_W)))
        yield
    put(P_BZ, proj(W_BZ, 256))
    yield
    put(P_GB, _silu(proj(W_BG, 256)))
    yield
    put(P_NQ, head_norm(proj(W_NQ, GROUP_W), nqg_ref[...]) * Q_SCALE)
    yield
    put(P_NK, head_norm(proj(W_NK, GROUP_W), nkg_ref[...]))
    yield
    put(P_NV, proj(W_NV, 256))
    yield
    put(P_GN, _silu(proj(W_NG, 256)))
    yield


def _projection(x, mods, mod_row, norm_gain, w_in, tables, gains, consts, tm):
    bsz, length, _ = x.shape
    cos, sa, sb = tables
    per_seq = length // tm
    n_tiles = bsz * per_seq
    norm_tile = lambda i: jnp.minimum(i, n_tiles - 1)
    out_tile = lambda i: jnp.maximum(i - 2, 0)
    vec = lambda k: pl.BlockSpec(
        (1, 1, D_MODEL), lambda i: (mod_row(norm_tile(i) // per_seq), 0, k))
    const2 = lambda a: pl.BlockSpec(a.shape, lambda i: (0, 0))
    tab = pl.BlockSpec((tm, LANES), lambda i: (out_tile(i) % per_seq, 0))
    hidden = pltpu.VMEM((tm, D_MODEL), BF16)
    projected = pltpu.VMEM((tm, IN_WIDTH), F32)
    return pl.pallas_call(
        _proj_kernel,
        grid=(n_tiles + 2,),
        in_specs=[
            pl.BlockSpec((1, tm, D_MODEL),
                         lambda i: (norm_tile(i) // per_seq, norm_tile(i) % per_seq, 0)),
            vec(0), vec(1), const2(norm_gain), const2(w_in), tab, tab, tab,
            *[const2(g) for g in gains], *[const2(a) for a in consts],
        ],
        out_specs=pl.BlockSpec((1, tm, P_WIDTH),
                               lambda i: (out_tile(i) // per_seq, out_tile(i) % per_seq, 0)),
        out_shape=jax.ShapeDtypeStruct((bsz, length, P_WIDTH), BF16),
        scratch_shapes=[hidden, hidden, projected, projected],
        compiler_params=pltpu.CompilerParams(
            vmem_limit_bytes=VMEM_LIMIT, dimension_semantics=("arbitrary",)),
        name="projection",
    )(x, mods, mods, norm_gain, w_in, cos, sa, sb, *gains, *consts)


def _attn_kernel(*refs, tq, tiles_per_row, n_tiles, tiles_per_iter, n_kv, chunks, n_bias,
                 sums_in_v, fp8):
    q_ref, g_ref = refs[0], refs[1]
    kv = refs[2:2 + 2 * n_kv]
    pos = 2 + 2 * n_kv
    bias_ref = refs[pos] if n_bias else None
    pos += 1 if n_bias else 0
    o_ref, s_a, s_b, m_a, m_b, mc_ref, lc_ref = refs[pos:pos + 7]
    mm_dtype = BF16
    q_scale = v_scale = None
    if fp8:
        mm_dtype = F8
        kv8 = refs[pos + 7:pos + 7 + 2 * n_kv]

        def amax(blocks):
            peaks = [jnp.max(jnp.abs(r[...].astype(F32)), axis=(0, 1, 2), keepdims=True)[0]
                     for r in blocks]
            return jnp.maximum(functools.reduce(jnp.maximum, peaks), TINY)

        def to_f8(v):
            return jnp.clip(v, -F8_MAX, F8_MAX).astype(F8)

        k_max, v_max, q_max = amax(kv[0::2]), amax(kv[1::2]), amax([q_ref])
        q_scale = jnp.exp2(jnp.round(0.5 * jnp.log2(k_max / q_max)))
        v_scale = jnp.exp2(jnp.ceil(jnp.log2(v_max * (1.0 / F8_TARGET))))
        v_scale = jnp.maximum(v_scale, 1.0 / F8_TARGET)
        for i, (src, dst) in enumerate(zip(kv, kv8)):
            inv = 1.0 / (v_scale if i % 2 else q_scale)
            dst[...] = to_f8(src[...].astype(F32) * inv)
        kv = kv8
    rows = HEADS_PER_GROUP * tq
    n_chunks = len(chunks)
    lane_head = jnp.right_shift(lax.broadcasted_iota(jnp.int32, (tq, GROUP_W), 1), 6)
    low_half = lax.broadcasted_iota(jnp.int32, (tq, LANES), 1) < HEAD_DIM
    shift = tiles_per_row.bit_length() - 1

    def tile_pos(t):
        if tiles_per_row == 1:
            return t, 0
        r0 = (t & (tiles_per_row - 1)) * tq
        return t >> shift, (r0 if isinstance(r0, int) else pl.multiple_of(r0, tq))

    def stage(t1, s1, m1, t0, s0, m0):
        if t1 is not None:
            b1, r1 = tile_pos(t1)
            q = q_ref[b1, pl.ds(r1, tq), :].astype(F32)
            if fp8:
                q = jnp.clip(q * q_scale, -F8_MAX, F8_MAX)
            qs = jnp.concatenate([jnp.where(lane_head == h, q, 0.0).astype(mm_dtype)
                                  for h in range(HEADS_PER_GROUP)], axis=0)
        if t0 is not None:
            b0, r0 = tile_pos(t0)
            m_prev = m0[...]
            m_prev = jnp.concatenate([m_prev, m_prev], axis=-1)
            acc = None
        for ci, (ai, off) in enumerate(chunks):
            if t1 is not None:
                k = kv[2 * ai][b1, off:off + KV_CHUNK, :]
                s = lax.dot_general(qs, k, (((1,), (1,)), ((), ())),
                                    preferred_element_type=F32)
                if ci < n_bias:
                    s = s + bias_ref[0, :, ci * KV_CHUNK:(ci + 1) * KV_CHUNK]
                s1[ci] = s
                mc_ref[ci] = jnp.maximum(s[:, :LANES], s[:, LANES:])
            if t0 is not None:
                z = s0[ci] - m_prev
                p = jnp.exp2(z.astype(BF16) if fp8 else z)
                if not sums_in_v:
                    lc_ref[ci] = (p[:, :LANES] + p[:, LANES:]).astype(F32)
                d = _dot(p.astype(mm_dtype), kv[2 * ai + 1][b0, off:off + KV_CHUNK, :])
                acc = d if acc is None else acc + d
        if t1 is not None:
            m = mc_ref[0]
            for ci in range(1, n_chunks):
                m = jnp.maximum(m, mc_ref[ci])
            m = jnp.max(m, axis=-1, keepdims=True) - (P_SHIFT if fp8 else 0.0)
            m1[...] = jnp.broadcast_to(m, (rows, LANES))
        if t0 is not None:
            gate = g_ref[b0, pl.ds(r0, tq), :].astype(F32)
            head = lambda a, h: a[h * tq:(h + 1) * tq, :]
            if sums_in_v:
                o = acc / pltpu.roll(acc, HEAD_DIM, 1)
                halves = [jnp.where(low_half, head(o, h), pltpu.roll(head(o, h + 1), HEAD_DIM, 1))
                          for h in (0, 2)]
                out = jnp.concatenate(halves, axis=-1)
            else:
                l = lc_ref[0]
                for ci in range(1, n_chunks):
                    l = l + lc_ref[ci]
                o = acc / jnp.sum(l, axis=-1, keepdims=True)
                if fp8:
                    o = o * v_scale
                out = jnp.zeros((tq, GROUP_W), F32)
                for h in range(HEADS_PER_GROUP):
                    out = out + jnp.where(lane_head == h, head(o, h), 0.0)
            o_ref[b0, pl.ds(r0, tq), :] = (out * gate).astype(BF16)

    stage(0, s_a, m_a, None, None, None)

    if tiles_per_iter == n_tiles:
        for t in range(0, n_tiles, 2):
            stage(t + 1, s_b, m_b, t, s_a, m_a)
            stage(t + 2 if t + 2 < n_tiles else None, s_a, m_a, t + 1, s_b, m_b)
        return

    def body(j, carry):
        t = tiles_per_iter * j
        for u in range(0, tiles_per_iter, 2):
            stage(t + u + 1, s_b, m_b, t + u, s_a, m_a)
            stage(jnp.minimum(t + u + 2, n_tiles - 1), s_a, m_a, t + u + 1, s_b, m_b)
        return carry

    lax.fori_loop(0, n_tiles // tiles_per_iter, body, 0)


def _attention(grid, nb, q_rows, q, gate, kvs, bias, out_shape, out_map, tq, chunks, name,
               fp8=False):
    n_kv = len(kvs) // 2
    n_bias = 0
    operands = [q[0], gate[0]]
    specs = [pl.BlockSpec((nb, q_rows, GROUP_W), q[1]),
             pl.BlockSpec((nb, q_rows, GROUP_W), gate[1])]
    for arr, kv_rows, kv_lanes, imap in kvs:
        operands.append(arr)
        specs.append(pl.BlockSpec((nb, kv_rows, kv_lanes), imap))
    sums_in_v = kvs[1][2] == LANES
    if bias is not None:
        arr, imap = bias
        n_bias = arr.shape[-1] // KV_CHUNK
        operands.append(arr)
        specs.append(pl.BlockSpec((1,) + arr.shape[1:], imap))
    rows = HEADS_PER_GROUP * tq
    tiles_per_row = q_rows // tq
    n_tiles = nb * tiles_per_row
    assert n_tiles % 2 == 0 and tiles_per_row & (tiles_per_row - 1) == 0
    score_buf = pltpu.VMEM((len(chunks), rows, KV_CHUNK), F32)
    max_buf = pltpu.VMEM((rows, LANES), F32)
    part_buf = pltpu.VMEM((len(chunks), rows, LANES), F32)
    sum_buf = pltpu.VMEM((1, 8, LANES) if sums_in_v else (len(chunks), rows, LANES), F32)
    kv8 = [pltpu.VMEM((nb, kv_rows, kv_lanes), F8) for _, kv_rows, kv_lanes, _ in kvs] if fp8 else []
    return pl.pallas_call(
        functools.partial(_attn_kernel, tq=tq, tiles_per_row=tiles_per_row, n_tiles=n_tiles,
                          tiles_per_iter=math.gcd(n_tiles, TILES_PER_ITER), n_kv=n_kv,
                          chunks=chunks, n_bias=n_bias, sums_in_v=sums_in_v, fp8=fp8),
        grid=grid,
        in_specs=specs,
        out_specs=pl.BlockSpec((nb, q_rows, GROUP_W), out_map),
        out_shape=jax.ShapeDtypeStruct(out_shape, BF16),
        scratch_shapes=[score_buf, score_buf, max_buf, max_buf, part_buf, sum_buf, *kv8],
        compiler_params=pltpu.CompilerParams(vmem_limit_bytes=VMEM_LIMIT),
        name=name,
    )(*operands)


def _col(off):
    return off // GROUP_W


GQA_TQ = 128
CTX_NB = 8


def _gqa_kv(p):
    rows = p.shape[1]
    return [(p, rows, GROUP_W, lambda b, j: (b, 0, _col(P_KE) + j)),
            (p, rows, LANES, lambda b, j: (b, 0, P_VE // LANES + j))]


def _gqa_latent(p, p_ctx, fp8):
    bsz, length, _ = p.shape
    n_ctx = p_ctx.shape[1]
    kvs = _gqa_kv(p) + _gqa_kv(p_ctx)
    chunks = tuple((0, c * KV_CHUNK) for c in range(length // KV_CHUNK)) + tuple(
        (1, c * KV_CHUNK) for c in range(n_ctx // KV_CHUNK))
    return _attention(
        grid=(bsz, 2), nb=1, q_rows=length,
        q=(p, lambda b, j: (b, 0, _col(P_QA) + j)),
        gate=(p, lambda b, j: (b, 0, _col(P_GA) + j)),
        kvs=kvs, bias=None,
        out_shape=(bsz, length, 2 * GROUP_W), out_map=lambda b, j: (b, 0, j),
        tq=GQA_TQ, chunks=chunks, name="gqa_attention", fp8=fp8)


def _gqa_context(p_ctx, fp8):
    bsz, n_ctx, _ = p_ctx.shape
    return _attention(
        grid=(bsz // CTX_NB, 2), nb=CTX_NB, q_rows=n_ctx,
        q=(p_ctx, lambda b, j: (b, 0, _col(P_QA) + j)),
        gate=(p_ctx, lambda b, j: (b, 0, _col(P_GA) + j)),
        kvs=_gqa_kv(p_ctx), bias=None,
        out_shape=(bsz, n_ctx, 2 * GROUP_W), out_map=lambda b, j: (b, 0, j),
        tq=GQA_TQ, chunks=tuple((0, c * KV_CHUNK) for c in range(n_ctx // KV_CHUNK)),
        name="gqa_context_attention", fp8=fp8)


NA_TQ = 256
NA_BAND = 3
NA_GROUPS = 2048 // NA_TQ
NA_NB = 4
NA_PATTERN_GROUPS = (0, 1, NA_GROUPS - 1)


def _na_band_start(g):
    return jnp.clip(g - 1, 0, NA_GROUPS - NA_BAND)


def _na_latent(p, p_ctx, bias, fp8):
    bsz, length, _ = p.shape
    kvs = []
    for i in range(NA_BAND):
        kvs.append((p, KV_CHUNK, GROUP_W,
                    lambda g, b, i=i: (b, _na_band_start(g) + i, _col(P_NK))))
        kvs.append((p, KV_CHUNK, GROUP_W,
                    lambda g, b, i=i: (b, _na_band_start(g) + i, _col(P_NV))))
    n_ctx = p_ctx.shape[1]
    kvs += [(p_ctx, n_ctx, GROUP_W, lambda g, b: (b, 0, _col(P_NK))),
            (p_ctx, n_ctx, GROUP_W, lambda g, b: (b, 0, _col(P_NV)))]
    chunks = tuple((i, 0) for i in range(NA_BAND)) + tuple(
        (NA_BAND, c * KV_CHUNK) for c in range(n_ctx // KV_CHUNK))
    pattern = lambda g, b: (jnp.where(g == 0, 0, jnp.where(g == NA_GROUPS - 1, 2, 1)), 0, 0)
    return _attention(
        grid=(NA_GROUPS, bsz // NA_NB), nb=NA_NB, q_rows=NA_TQ,
        q=(p, lambda g, b: (b, g, _col(P_NQ))),
        gate=(p, lambda g, b: (b, g, _col(P_GN))),
        kvs=kvs, bias=(bias, pattern),
        out_shape=(bsz, length, GROUP_W), out_map=lambda g, b: (b, g, 0),
        tq=NA_TQ, chunks=chunks, name="na_attention", fp8=fp8)


def _na_context(p_ctx, fp8):
    bsz, n_ctx, _ = p_ctx.shape
    kvs = [(p_ctx, n_ctx, GROUP_W, lambda b: (b, 0, _col(P_NK))),
           (p_ctx, n_ctx, GROUP_W, lambda b: (b, 0, _col(P_NV)))]
    return _attention(
        grid=(bsz // CTX_NB,), nb=CTX_NB, q_rows=n_ctx,
        q=(p_ctx, lambda b: (b, 0, _col(P_NQ))),
        gate=(p_ctx, lambda b: (b, 0, _col(P_GN))),
        kvs=kvs, bias=None,
        out_shape=(bsz, n_ctx, GROUP_W), out_map=lambda b: (b, 0, 0),
        tq=NA_TQ, chunks=tuple((0, c * KV_CHUNK) for c in range(n_ctx // KV_CHUNK)),
        name="na_context_attention", fp8=fp8)


def _na_bias_layout():
    rows = 2048 // GRID_W
    rows_per_tile, rows_per_chunk = NA_TQ // GRID_W, KV_CHUNK // GRID_W
    band_rows = NA_BAND * rows_per_chunk
    slot = np.full((len(NA_PATTERN_GROUPS), rows_per_tile, band_rows), 2 * NA_KH - 1)
    for pat, g in enumerate(NA_PATTERN_GROUPS):
        band0 = rows_per_chunk * min(max(g - 1, 0), NA_GROUPS - NA_BAND)
        for qr in range(rows_per_tile):
            r = rows_per_tile * g + qr
            r0 = min(max(r - NA_KH // 2, 0), rows - NA_KH)
            for kr in range(band_rows):
                rp = band0 + kr
                if r0 <= rp < r0 + NA_KH:
                    slot[pat, qr, kr] = rp - r + NA_KH - 1
    c = np.arange(GRID_W)
    c0 = np.clip(c - NA_KW // 2, 0, GRID_W - NA_KW)
    col_ok = (c[None, :] >= c0[:, None]) & (c[None, :] < c0[:, None] + NA_KW)
    return slot, col_ok


def _na_bias(rpb, slot, col_ok):
    heads, n_dr, _ = rpb.shape
    rpb = rpb * LOG2E
    w = jnp.concatenate([rpb[..., NA_KW - 1:],
                         jnp.full((heads, n_dr, LANES - (2 * NA_KW - 1)), NEG, F32),
                         rpb[..., :NA_KW - 1]], axis=-1)
    t = jnp.tile(w, (1, 1, GRID_W))[..., :GRID_W * (LANES - 1)]
    t = t.reshape(heads, n_dr, GRID_W, LANES - 1)[..., :GRID_W]
    t = jnp.where(col_ok[None, None], t, NEG)
    t = jnp.concatenate([t, jnp.full((heads, 1, GRID_W, GRID_W), NEG, F32)], axis=1)
    n_pat, n_qr, n_kr = slot.shape
    blocks = jnp.stack([t[:, int(s)] for s in slot.reshape(-1)], axis=1)
    blocks = blocks.reshape(heads, n_pat, n_qr, n_kr, GRID_W, GRID_W)
    blocks = jnp.transpose(blocks, (1, 0, 2, 4, 3, 5))
    return blocks.reshape(n_pat, heads * n_qr * GRID_W, n_kr * GRID_W)


POOL_HALO = 8
POOL_ROWS = 256


def _pool_kernel(z_ref, g_ref, cnt_ref, w_ref, s_ref, o_ref, zp_ref, *, length):
    zeros = jnp.zeros((POOL_HALO, GROUP_W), F32)
    zp_ref[0:POOL_HALO, :] = zeros
    zp_ref[POOL_HALO + length:2 * POOL_HALO + length, :] = zeros
    zp_ref[POOL_HALO:POOL_HALO + length, :] = z_ref[0].astype(F32)
    first = lax.broadcasted_iota(jnp.int32, (POOL_ROWS, LANES), 1) < HEAD_DIM

    def window(t0, lanes, offsets):
        acc = None
        for o in offsets:
            v = zp_ref[POOL_HALO + t0 + o:POOL_HALO + t0 + o + POOL_ROWS, lanes]
            acc = v if acc is None else acc + v
        return acc

    for t0 in range(0, length, POOL_ROWS):
        lo, hi = slice(0, LANES), slice(LANES, 2 * LANES)
        s2 = window(t0, lo, (-1, 0))
        s4 = s2 + window(t0, lo, (-2, 1))
        s8 = window(t0, hi, range(-4, 4))
        s16 = s8 + window(t0, hi, tuple(range(-8, -4)) + tuple(range(4, 8)))
        win = jnp.concatenate([jnp.where(first, s2, s4), jnp.where(first, s8, s16)], axis=-1)
        rows = slice(t0, t0 + POOL_ROWS)
        z = zp_ref[POOL_HALO + t0:POOL_HALO + t0 + POOL_ROWS, :]
        pooled = win / cnt_ref[rows, :] - z
        y = _dot(pooled.astype(BF16), w_ref[...]) * s_ref[...]
        o_ref[0, rows, :] = (y * g_ref[0, rows, :].astype(F32)).astype(BF16)


def _pool(p, cnt, w_bd, scale):
    bsz, length, _ = p.shape
    return pl.pallas_call(
        functools.partial(_pool_kernel, length=length),
        grid=(bsz,),
        in_specs=[
            pl.BlockSpec((1, length, GROUP_W), lambda b: (b, 0, _col(P_BZ))),
            pl.BlockSpec((1, length, GROUP_W), lambda b: (b, 0, _col(P_GB))),
            pl.BlockSpec((length, GROUP_W), lambda b: (0, 0)),
            pl.BlockSpec((GROUP_W, GROUP_W), lambda b: (0, 0)),
            pl.BlockSpec((1, GROUP_W), lambda b: (0, 0)),
        ],
        out_specs=pl.BlockSpec((1, length, GROUP_W), lambda b: (b, 0, 0)),
        out_shape=jax.ShapeDtypeStruct((bsz, length, GROUP_W), BF16),
        scratch_shapes=[pltpu.VMEM((length + 2 * POOL_HALO, GROUP_W), F32)],
        compiler_params=pltpu.CompilerParams(vmem_limit_bytes=VMEM_LIMIT),
        name="pool_mixer",
    )(p, p, cnt, w_bd, scale)


def _pool_counts(length):
    t = np.arange(length)
    cols = []
    for w in POOL_WINDOWS:
        lo = np.maximum(t - w // 2, 0)
        hi = np.minimum(t + w // 2 - 1, length - 1)
        cols.append(np.repeat((hi - lo + 1).astype(np.float32)[:, None], HEAD_DIM, axis=1))
    return np.concatenate(cols, axis=1)


def _out_kernel(x_ref, gate_ref, a_ref, b_ref, n_ref, w_ref, o_ref):
    y = (_dot(a_ref[0], w_ref[0:512, :]) + _dot(b_ref[0], w_ref[512:768, :])
         + _dot(n_ref[0], w_ref[768:1024, :]))
    o_ref[0] = x_ref[0] + gate_ref[0] * y


def _output(x, mods, mod_row, a, b, n, w_out, tm):
    bsz, length, _ = x.shape
    tok = lambda width: pl.BlockSpec((1, tm, width), lambda bb, i: (bb, i, 0))
    return pl.pallas_call(
        _out_kernel,
        grid=(bsz, length // tm),
        in_specs=[
            tok(D_MODEL),
            pl.BlockSpec((1, 1, D_MODEL), lambda bb, i: (mod_row(bb), 0, 2)),
            tok(2 * GROUP_W), tok(GROUP_W), tok(GROUP_W),
            pl.BlockSpec((D_MODEL, D_MODEL), lambda bb, i: (0, 0)),
        ],
        out_specs=tok(D_MODEL),
        out_shape=jax.ShapeDtypeStruct(x.shape, F32),
        compiler_params=pltpu.CompilerParams(vmem_limit_bytes=VMEM_LIMIT),
        name="output_projection",
    )(x, mods, a, b, n, w_out)


def _rope_tables(seq):
    t = jnp.arange(seq)
    half = HEAD_DIM // 4
    inv_freq = ROPE_THETA ** (-jnp.arange(half, dtype=jnp.float32) / half)
    cos, sa, sb = [], [], []
    zero = jnp.zeros((seq, half), F32)
    for pos in (t // GRID_W, t % GRID_W):
        ang = pos.astype(jnp.float32)[:, None] * inv_freq[None, :]
        cos += [jnp.cos(ang), jnp.cos(ang)]
        sa += [-jnp.sin(ang), zero]
        sb += [zero, jnp.sin(ang)]
    tile = lambda parts: jnp.tile(jnp.concatenate(parts, axis=-1), (1, LANES // HEAD_DIM))
    return tile(cos), tile(sa), tile(sb)


def _layout_constants():
    lane = np.arange(GROUP_W)
    seg = lane[:, None] // HEAD_DIM == lane[None, :] // HEAD_DIM
    src = np.arange(LANES)[:, None]
    col = np.arange(2 * GROUP_W)[None, :]
    rep_k = src == (col // GROUP_W) * HEAD_DIM + col % HEAD_DIM
    colv = np.arange(2 * LANES)[None, :]
    is_v = colv % LANES < HEAD_DIM
    rep_v = is_v & (src == (colv // LANES) * HEAD_DIM + colv % LANES)
    return (jnp.asarray(seg, BF16), jnp.asarray(rep_k, BF16), jnp.asarray(rep_v, BF16),
            jnp.asarray(~is_v, F32))


def kernel(x, c, ctx, c_ctx, norm_gain, w_mod, b_mod, w_in, att_q_gain, att_k_gain,
           pool_w, pool_scale, na_q_gain, na_k_gain, na_rpb, w_out):
    bsz, seq, _ = x.shape
    n_ctx = ctx.shape[1]
    assert seq == NA_GROUPS * NA_TQ and n_ctx % KV_CHUNK == 0 and bsz < MOD_ROWS
    assert bsz % CTX_NB == 0 and bsz % NA_NB == 0

    mods_all = _modulation(c, c_ctx, w_mod, b_mod)
    tables = _rope_tables(seq)
    tables_ctx = (jnp.ones((n_ctx, LANES), F32), jnp.zeros((n_ctx, LANES), F32),
                  jnp.zeros((n_ctx, LANES), F32))
    consts = _layout_constants()
    cnt, cnt_ctx = jnp.asarray(_pool_counts(seq)), jnp.asarray(_pool_counts(n_ctx))
    slot, col_ok = _na_bias_layout()
    w_in_b = w_in.astype(BF16)
    w_out_b = w_out.astype(BF16)
    wide = lambda g: jnp.tile(g, GROUP_W // HEAD_DIM)[None, :]
    lat_row = lambda b: b
    ctx_row = lambda b: bsz

    for l in range(DEPTH):
        last = l == DEPTH - 1
        mods = mods_all[l].reshape(MOD_ROWS, 1, 3 * D_MODEL)
        gains = (wide(att_q_gain[l]), wide(att_k_gain[l]), wide(na_q_gain[l]),
                 wide(na_k_gain[l]))
        ng = norm_gain[l][None, :]
        p = _projection(x, mods, lat_row, ng, w_in_b[l], tables, gains, consts, TOKEN_TILE)
        p_ctx = _projection(ctx, mods, ctx_row, ng, w_in_b[l], tables_ctx, gains, consts, 256)
        bias = _na_bias(na_rpb[l], slot, col_ok)
        w_bd = jax.scipy.linalg.block_diag(*[pool_w[l, g] for g in range(4)]).astype(BF16)
        ps = pool_scale[l][None, :]

        peak = lambda g: jnp.max(jnp.abs(g[l]))
        gqa_fp8 = peak(att_q_gain) * peak(att_k_gain) <= FP8_GAIN_LIMIT
        na_fp8 = peak(na_q_gain) * peak(na_k_gain) <= FP8_GAIN_LIMIT
        either = lambda use_fp8, fn, *args: lax.cond(
            use_fp8, lambda: fn(*args, True), lambda: fn(*args, False))

        a = either(gqa_fp8, _gqa_latent, p, p_ctx)
        n = either(na_fp8, _na_latent, p, p_ctx, bias)
        bo = _pool(p, cnt, w_bd, ps)
        x_new = _output(x, mods, lat_row, a, bo, n, w_out_b[l], OUT_TILE)

        if not last:
            a_c = either(gqa_fp8, _gqa_context, p_ctx)
            n_c = either(na_fp8, _na_context, p_ctx)
            bo_c = _pool(p_ctx, cnt_ctx, w_bd, ps)
            ctx = _output(ctx, mods, ctx_row, a_c, bo_c, n_c, w_out_b[l], 256)
        x = x_new
    return x
```

```python
import functools
import math

import numpy as np
import jax
import jax.numpy as jnp
from jax import lax
from jax.experimental import pallas as pl
from jax.experimental.pallas import tpu as pltpu

F32 = jnp.float32
BF16 = jnp.bfloat16
F8 = jnp.float8_e4m3fn
P_SHIFT = 8.0
F8_MAX = 448.0
F8_TARGET = 256.0
TINY = 1e-30
FP8_GAIN_LIMIT = 2.0

D_MODEL = 1024
DEPTH = 4
GRID_W = 64
HEAD_DIM = 64
ROPE_THETA = 10000.0
EPS = 1e-6
ATTN_SCALE = HEAD_DIM ** -0.5
LOG2E = 1.4426950408889634
Q_SCALE = ATTN_SCALE * LOG2E
POOL_WINDOWS = (2, 4, 8, 16)
NA_KH = 8
NA_KW = 16
IN_WIDTH = 2816

LANES = 128
GROUP_W = 256
HEADS_PER_GROUP = GROUP_W // HEAD_DIM
KV_CHUNK = 256
TILES_PER_ITER = 16

W_AQ, W_AK, W_AV, W_AG, W_BZ, W_BG, W_NQ, W_NK, W_NV, W_NG = (
    0, 512, 640, 768, 1280, 1536, 1792, 2048, 2304, 2560)
P_QA, P_KE, P_VE, P_GA, P_BZ, P_GB, P_NQ, P_NK, P_NV, P_GN = (
    0, 512, 1024, 1280, 1792, 2048, 2304, 2560, 2816, 3072)
P_WIDTH = 3328
MOD_ROWS = 24
TOKEN_TILE = 512
OUT_TILE = 1024
NEG = -1e30
VMEM_LIMIT = 48 * 1024 * 1024


def _silu(v):
    return v / (1.0 + jnp.exp(-v))


def _split_bf16(v):
    hi = v.astype(BF16)
    lo = (v - hi.astype(F32)).astype(BF16)
    return hi, lo


def _dot(a, b):
    return jnp.dot(a, b, preferred_element_type=F32)


def _mod_kernel(c_ref, w_ref, b_ref, o_ref):
    a_hi, a_lo = _split_bf16(_silu(c_ref[...]))
    w_hi, w_lo = _split_bf16(w_ref[0])
    o_ref[0] = _dot(a_hi, w_hi) + _dot(a_lo, w_hi) + _dot(a_hi, w_lo) + b_ref[0]


def _modulation(c, c_ctx, w_mod, b_mod):
    bsz = c.shape[0]
    cs = jnp.concatenate(
        [c, c_ctx[None], jnp.zeros((MOD_ROWS - bsz - 1, D_MODEL), F32)], axis=0)
    return pl.pallas_call(
        _mod_kernel,
        grid=(DEPTH, 3),
        in_specs=[
            pl.BlockSpec((MOD_ROWS, D_MODEL), lambda l, j: (0, 0)),
            pl.BlockSpec((1, D_MODEL, D_MODEL), lambda l, j: (l, 0, j)),
            pl.BlockSpec((1, 1, D_MODEL), lambda l, j: (l, 0, j)),
        ],
        out_specs=pl.BlockSpec((1, MOD_ROWS, D_MODEL), lambda l, j: (l, 0, j)),
        out_shape=jax.ShapeDtypeStruct((DEPTH, MOD_ROWS, 3 * D_MODEL), F32),
        compiler_params=pltpu.CompilerParams(vmem_limit_bytes=VMEM_LIMIT),
        name="modulation",
    )(cs, w_mod, b_mod.reshape(DEPTH, 1, 3 * D_MODEL))


def _proj_kernel(x_ref, shift_ref, scale_ref, ng_ref, w_ref, cos_ref, sa_ref, sb_ref,
                 aqg_ref, akg_ref, nqg_ref, nkg_ref, seg_ref, repk_ref, repv_ref, ones_ref,
                 o_ref, hb0_ref, hb1_ref, y0_ref, y1_ref):
    step = pl.program_id(0)

    @pl.when(step == 0)
    def _():
        hb1_ref[...] = jnp.zeros_like(hb1_ref)
        y0_ref[...] = jnp.zeros_like(y0_ref)
        y1_ref[...] = jnp.zeros_like(y1_ref)

    def pre_norm(hb_ref):
        x = x_ref[0]
        ms = jnp.mean(x * x, axis=-1, keepdims=True)
        h = x * lax.rsqrt(ms + EPS) * ng_ref[...]
        h = h * (1.0 + scale_ref[0]) + shift_ref[0]
        hb_ref[...] = h.astype(BF16)

    def stage(hb_w, hb_r, y_w, y_r):
        pre_norm(hb_w)
        epilogue = _proj_epilogue(y_r, cos_ref, sa_ref, sb_ref, aqg_ref, akg_ref, nqg_ref,
                                  nkg_ref, seg_ref, repk_ref, repv_ref, ones_ref, o_ref)
        for c0 in range(0, IN_WIDTH, GROUP_W):
            y_w[:, c0:c0 + GROUP_W] = _dot(hb_r[...], w_ref[:, c0:c0 + GROUP_W])
            next(epilogue)

    @pl.when(step % 2 == 0)
    def _():
        stage(hb0_ref, hb1_ref, y1_ref, y0_ref)

    @pl.when(step % 2 == 1)
    def _():
        stage(hb1_ref, hb0_ref, y0_ref, y1_ref)


def _proj_epilogue(y_ref, cos_ref, sa_ref, sb_ref, aqg_ref, akg_ref, nqg_ref, nkg_ref,
                   seg_ref, repk_ref, repv_ref, ones_ref, o_ref):
    def proj(c0, width):
        return y_ref[:, c0:c0 + width]

    def head_norm(y, g):
        ss = _dot((y * y).astype(BF16), seg_ref[...])
        return y * lax.rsqrt(ss * (1.0 / HEAD_DIM) + EPS) * g

    def rope(y):
        width = y.shape[-1]
        wide = lambda t: jnp.tile(t[...], (1, width // LANES))
        return (y * wide(cos_ref) + pltpu.roll(y, width - 16, 1) * wide(sa_ref)
                + pltpu.roll(y, 16, 1) * wide(sb_ref))

    def put(c0, v):
        o_ref[0, :, c0:c0 + v.shape[-1]] = v.astype(BF16)

    for j in range(2):
        y = proj(W_AQ + j * GROUP_W, GROUP_W)
        put(P_QA + j * GROUP_W, rope(head_norm(y, aqg_ref[...])) * Q_SCALE)
        yield
    kv = proj(W_AK, GROUP_W)
    kb = rope(head_norm(kv, akg_ref[...]))[:, :LANES].astype(BF16)
    put(P_KE, _dot(kb, repk_ref[...]))
    put(P_VE, _dot(kv[:, LANES:].astype(BF16), repv_ref[...]) + ones_ref[...])
    yield
    for j in range(2):
        put(P_GA + j * GROUP_W, _silu(proj(W_AG + j * GROUP_W, GROUP_W)))
        yield
    put(P_BZ, proj(W_BZ, 256))
    yield
    put(P_GB, _silu(proj(W_BG, 256)))
    yield
    put(P_NQ, head_norm(proj(W_NQ, GROUP_W), nqg_ref[...]) * Q_SCALE)
    yield
    put(P_NK, head_norm(proj(W_NK, GROUP_W), nkg_ref[...]))
    yield
    put(P_NV, proj(W_NV, 256))
    yield
    put(P_GN, _silu(proj(W_NG, 256)))
    yield


def _projection(x, mods, mod_row, norm_gain, w_in, tables, gains, consts, tm):
    bsz, length, _ = x.shape
    cos, sa, sb = tables
    per_seq = length // tm
    n_tiles = bsz * per_seq
    norm_tile = lambda i: jnp.minimum(i, n_tiles - 1)
    out_tile = lambda i: jnp.maximum(i - 2, 0)
    vec = lambda k: pl.BlockSpec(
        (1, 1, D_MODEL), lambda i: (mod_row(norm_tile(i) // per_seq), 0, k))
    const2 = lambda a: pl.BlockSpec(a.shape, lambda i: (0, 0))
    tab = pl.BlockSpec((tm, LANES), lambda i: (out_tile(i) % per_seq, 0))
    hidden = pltpu.VMEM((tm, D_MODEL), BF16)
    projected = pltpu.VMEM((tm, IN_WIDTH), F32)
    return pl.pallas_call(
        _proj_kernel,
        grid=(n_tiles + 2,),
        in_specs=[
            pl.BlockSpec((1, tm, D_MODEL),
                         lambda i: (norm_tile(i) // per_seq, norm_tile(i) % per_seq, 0)),
            vec(0), vec(1), const2(norm_gain), const2(w_in), tab, tab, tab,
            *[const2(g) for g in gains], *[const2(a) for a in consts],
        ],
        out_specs=pl.BlockSpec((1, tm, P_WIDTH),
                               lambda i: (out_tile(i) // per_seq, out_tile(i) % per_seq, 0)),
        out_shape=jax.ShapeDtypeStruct((bsz, length, P_WIDTH), BF16),
        scratch_shapes=[hidden, hidden, projected, projected],
        compiler_params=pltpu.CompilerParams(
            vmem_limit_bytes=VMEM_LIMIT, dimension_semantics=("arbitrary",)),
        name="projection",
    )(x, mods, mods, norm_gain, w_in, cos, sa, sb, *gains, *consts)


def _attn_kernel(*refs, tq, tiles_per_row, n_tiles, tiles_per_iter, n_kv, chunks, n_bias,
                 sums_in_v, fp8):
    q_ref, g_ref = refs[0], refs[1]
    kv = refs[2:2 + 2 * n_kv]
    pos = 2 + 2 * n_kv
    bias_ref = refs[pos] if n_bias else None
    pos += 1 if n_bias else 0
    o_ref, s_a, s_b, m_a, m_b, mc_ref, lc_ref = refs[pos:pos + 7]
    mm_dtype = BF16
    q_scale = v_scale = None
    if fp8:
        mm_dtype = F8
        kv8 = refs[pos + 7:pos + 7 + 2 * n_kv]

        def amax(blocks):
            peaks = [jnp.max(jnp.max(jnp.abs(r[...]), axis=1).astype(F32), keepdims=True)
                     for r in blocks]
            return jnp.maximum(functools.reduce(jnp.maximum, peaks), TINY)

        k_max, v_max, q_max = amax(kv[0::2]), amax(kv[1::2]), amax([q_ref])
        q_scale = jnp.exp2(jnp.round(0.5 * jnp.log2(k_max / q_max)))
        v_scale = jnp.exp2(jnp.ceil(jnp.log2(v_max * (1.0 / F8_TARGET))))
        v_scale = jnp.maximum(v_scale, 1.0 / F8_TARGET)
        for i, (src, dst) in enumerate(zip(kv, kv8)):
            inv = (1.0 / (v_scale if i % 2 else q_scale)).astype(BF16)
            dst[...] = jnp.clip(src[...] * inv, -F8_MAX, F8_MAX).astype(F8)
        kv = kv8
    rows = HEADS_PER_GROUP * tq
    n_chunks = len(chunks)
    lane_head = jnp.right_shift(lax.broadcasted_iota(jnp.int32, (tq, GROUP_W), 1), 6)
    low_half = lax.broadcasted_iota(jnp.int32, (tq, LANES), 1) < HEAD_DIM
    shift = tiles_per_row.bit_length() - 1

    def tile_pos(t):
        if tiles_per_row == 1:
            return t, 0
        r0 = (t & (tiles_per_row - 1)) * tq
        return t >> shift, (r0 if isinstance(r0, int) else pl.multiple_of(r0, tq))

    def stage(t1, s1, m1, t0, s0, m0):
        if t1 is not None:
            b1, r1 = tile_pos(t1)
            q = q_ref[b1, pl.ds(r1, tq), :].astype(F32)
            if fp8:
                q = jnp.clip(q * q_scale, -F8_MAX, F8_MAX)
            qs = jnp.concatenate([jnp.where(lane_head == h, q, 0.0).astype(mm_dtype)
                                  for h in range(HEADS_PER_GROUP)], axis=0)
        if t0 is not None:
            b0, r0 = tile_pos(t0)
            m_prev = m0[...]
            m_prev = jnp.concatenate([m_prev, m_prev], axis=-1)
            acc = None
        for ci, (ai, off) in enumerate(chunks):
            if t1 is not None:
                k = kv[2 * ai][b1, off:off + KV_CHUNK, :]
                s = lax.dot_general(qs, k, (((1,), (1,)), ((), ())),
                                    preferred_element_type=F32)
                if ci < n_bias:
                    s = s + bias_ref[0, :, ci * KV_CHUNK:(ci + 1) * KV_CHUNK]
                s1[ci] = s
                mc_ref[ci] = jnp.maximum(s[:, :LANES], s[:, LANES:])
            if t0 is not None:
                z = s0[ci] - m_prev
                p = jnp.exp2(z.astype(BF16) if fp8 else z)
                if not sums_in_v:
                    lc_ref[ci] = (p[:, :LANES] + p[:, LANES:]).astype(F32)
                d = _dot(p.astype(mm_dtype), kv[2 * ai + 1][b0, off:off + KV_CHUNK, :])
                acc = d if acc is None else acc + d
        if t1 is not None:
            m = mc_ref[0]
            for ci in range(1, n_chunks):
                m = jnp.maximum(m, mc_ref[ci])
            m = jnp.max(m, axis=-1, keepdims=True) - (P_SHIFT if fp8 else 0.0)
            m1[...] = jnp.broadcast_to(m, (rows, LANES))
        if t0 is not None:
            gate = g_ref[b0, pl.ds(r0, tq), :].astype(F32)
            head = lambda a, h: a[h * tq:(h + 1) * tq, :]
            if sums_in_v:
                o = acc / pltpu.roll(acc, HEAD_DIM, 1)
                halves = [jnp.where(low_half, head(o, h), pltpu.roll(head(o, h + 1), HEAD_DIM, 1))
                          for h in (0, 2)]
                out = jnp.concatenate(halves, axis=-1)
            else:
                l = lc_ref[0]
                for ci in range(1, n_chunks):
                    l = l + lc_ref[ci]
                o = acc / jnp.sum(l, axis=-1, keepdims=True)
                if fp8:
                    o = o * v_scale
                out = jnp.zeros((tq, GROUP_W), F32)
                for h in range(HEADS_PER_GROUP):
                    out = out + jnp.where(lane_head == h, head(o, h), 0.0)
            o_ref[b0, pl.ds(r0, tq), :] = (out * gate).astype(BF16)

    stage(0, s_a, m_a, None, None, None)

    if tiles_per_iter == n_tiles:
        for t in range(0, n_tiles, 2):
            stage(t + 1, s_b, m_b, t, s_a, m_a)
            stage(t + 2 if t + 2 < n_tiles else None, s_a, m_a, t + 1, s_b, m_b)
        return

    def body(j, carry):
        t = tiles_per_iter * j
        for u in range(0, tiles_per_iter, 2):
            stage(t + u + 1, s_b, m_b, t + u, s_a, m_a)
            stage(jnp.minimum(t + u + 2, n_tiles - 1), s_a, m_a, t + u + 1, s_b, m_b)
        return carry

    lax.fori_loop(0, n_tiles // tiles_per_iter, body, 0)


def _attention(grid, nb, q_rows, q, gate, kvs, bias, out_shape, out_map, tq, chunks, name,
               fp8=False):
    n_kv = len(kvs) // 2
    n_bias = 0
    operands = [q[0], gate[0]]
    specs = [pl.BlockSpec((nb, q_rows, GROUP_W), q[1]),
             pl.BlockSpec((nb, q_rows, GROUP_W), gate[1])]
    for arr, kv_rows, kv_lanes, imap in kvs:
        operands.append(arr)
        specs.append(pl.BlockSpec((nb, kv_rows, kv_lanes), imap))
    sums_in_v = kvs[1][2] == LANES
    if bias is not None:
        arr, imap = bias
        n_bias = arr.shape[-1] // KV_CHUNK
        operands.append(arr)
        specs.append(pl.BlockSpec((1,) + arr.shape[1:], imap))
    rows = HEADS_PER_GROUP * tq
    tiles_per_row = q_rows // tq
    n_tiles = nb * tiles_per_row
    assert n_tiles % 2 == 0 and tiles_per_row & (tiles_per_row - 1) == 0
    score_buf = pltpu.VMEM((len(chunks), rows, KV_CHUNK), F32)
    max_buf = pltpu.VMEM((rows, LANES), F32)
    part_buf = pltpu.VMEM((len(chunks), rows, LANES), F32)
    sum_buf = pltpu.VMEM((1, 8, LANES) if sums_in_v else (len(chunks), rows, LANES), F32)
    kv8 = [pltpu.VMEM((nb, kv_rows, kv_lanes), F8) for _, kv_rows, kv_lanes, _ in kvs] if fp8 else []
    return pl.pallas_call(
        functools.partial(_attn_kernel, tq=tq, tiles_per_row=tiles_per_row, n_tiles=n_tiles,
                          tiles_per_iter=math.gcd(n_tiles, TILES_PER_ITER), n_kv=n_kv,
                          chunks=chunks, n_bias=n_bias, sums_in_v=sums_in_v, fp8=fp8),
        grid=grid,
        in_specs=specs,
        out_specs=pl.BlockSpec((nb, q_rows, GROUP_W), out_map),
        out_shape=jax.ShapeDtypeStruct(out_shape, BF16),
        scratch_shapes=[score_buf, score_buf, max_buf, max_buf, part_buf, sum_buf, *kv8],
        compiler_params=pltpu.CompilerParams(vmem_limit_bytes=VMEM_LIMIT),
        name=name,
    )(*operands)


def _col(off):
    return off // GROUP_W


GQA_TQ = 128
CTX_NB = 8


def _gqa_kv(p):
    rows = p.shape[1]
    return [(p, rows, GROUP_W, lambda b, j: (b, 0, _col(P_KE) + j)),
            (p, rows, LANES, lambda b, j: (b, 0, P_VE // LANES + j))]


def _gqa_latent(p, p_ctx, fp8):
    bsz, length, _ = p.shape
    n_ctx = p_ctx.shape[1]
    kvs = _gqa_kv(p) + _gqa_kv(p_ctx)
    chunks = tuple((0, c * KV_CHUNK) for c in range(length // KV_CHUNK)) + tuple(
        (1, c * KV_CHUNK) for c in range(n_ctx // KV_CHUNK))
    return _attention(
        grid=(bsz, 2), nb=1, q_rows=length,
        q=(p, lambda b, j: (b, 0, _col(P_QA) + j)),
        gate=(p, lambda b, j: (b, 0, _col(P_GA) + j)),
        kvs=kvs, bias=None,
        out_shape=(bsz, length, 2 * GROUP_W), out_map=lambda b, j: (b, 0, j),
        tq=GQA_TQ, chunks=chunks, name="gqa_attention", fp8=fp8)


def _gqa_context(p_ctx, fp8):
    bsz, n_ctx, _ = p_ctx.shape
    return _attention(
        grid=(bsz // CTX_NB, 2), nb=CTX_NB, q_rows=n_ctx,
        q=(p_ctx, lambda b, j: (b, 0, _col(P_QA) + j)),
        gate=(p_ctx, lambda b, j: (b, 0, _col(P_GA) + j)),
        kvs=_gqa_kv(p_ctx), bias=None,
        out_shape=(bsz, n_ctx, 2 * GROUP_W), out_map=lambda b, j: (b, 0, j),
        tq=GQA_TQ, chunks=tuple((0, c * KV_CHUNK) for c in range(n_ctx // KV_CHUNK)),
        name="gqa_context_attention", fp8=fp8)


NA_TQ = 256
NA_BAND = 3
NA_GROUPS = 2048 // NA_TQ
NA_NB = 4
NA_PATTERN_GROUPS = (0, 1, NA_GROUPS - 1)


def _na_band_start(g):
    return jnp.clip(g - 1, 0, NA_GROUPS - NA_BAND)


def _na_latent(p, p_ctx, bias, fp8):
    bsz, length, _ = p.shape
    kvs = []
    for i in range(NA_BAND):
        kvs.append((p, KV_CHUNK, GROUP_W,
                    lambda g, b, i=i: (b, _na_band_start(g) + i, _col(P_NK))))
        kvs.append((p, KV_CHUNK, GROUP_W,
                    lambda g, b, i=i: (b, _na_band_start(g) + i, _col(P_NV))))
    n_ctx = p_ctx.shape[1]
    kvs += [(p_ctx, n_ctx, GROUP_W, lambda g, b: (b, 0, _col(P_NK))),
            (p_ctx, n_ctx, GROUP_W, lambda g, b: (b, 0, _col(P_NV)))]
    chunks = tuple((i, 0) for i in range(NA_BAND)) + tuple(
        (NA_BAND, c * KV_CHUNK) for c in range(n_ctx // KV_CHUNK))
    pattern = lambda g, b: (jnp.where(g == 0, 0, jnp.where(g == NA_GROUPS - 1, 2, 1)), 0, 0)
    return _attention(
        grid=(NA_GROUPS, bsz // NA_NB), nb=NA_NB, q_rows=NA_TQ,
        q=(p, lambda g, b: (b, g, _col(P_NQ))),
        gate=(p, lambda g, b: (b, g, _col(P_GN))),
        kvs=kvs, bias=(bias, pattern),
        out_shape=(bsz, length, GROUP_W), out_map=lambda g, b: (b, g, 0),
        tq=NA_TQ, chunks=chunks, name="na_attention", fp8=fp8)


def _na_context(p_ctx, fp8):
    bsz, n_ctx, _ = p_ctx.shape
    kvs = [(p_ctx, n_ctx, GROUP_W, lambda b: (b, 0, _col(P_NK))),
           (p_ctx, n_ctx, GROUP_W, lambda b: (b, 0, _col(P_NV)))]
    return _attention(
        grid=(bsz // CTX_NB,), nb=CTX_NB, q_rows=n_ctx,
        q=(p_ctx, lambda b: (b, 0, _col(P_NQ))),
        gate=(p_ctx, lambda b: (b, 0, _col(P_GN))),
        kvs=kvs, bias=None,
        out_shape=(bsz, n_ctx, GROUP_W), out_map=lambda b: (b, 0, 0),
        tq=NA_TQ, chunks=tuple((0, c * KV_CHUNK) for c in range(n_ctx // KV_CHUNK)),
        name="na_context_attention", fp8=fp8)


def _na_bias_layout():
    rows = 2048 // GRID_W
    rows_per_tile, rows_per_chunk = NA_TQ // GRID_W, KV_CHUNK // GRID_W
    band_rows = NA_BAND * rows_per_chunk
    slot = np.full((len(NA_PATTERN_GROUPS), rows_per_tile, band_rows), 2 * NA_KH - 1)
    for pat, g in enumerate(NA_PATTERN_GROUPS):
        band0 = rows_per_chunk * min(max(g - 1, 0), NA_GROUPS - NA_BAND)
        for qr in range(rows_per_tile):
            r = rows_per_tile * g + qr
            r0 = min(max(r - NA_KH // 2, 0), rows - NA_KH)
            for kr in range(band_rows):
                rp = band0 + kr
                if r0 <= rp < r0 + NA_KH:
                    slot[pat, qr, kr] = rp - r + NA_KH - 1
    c = np.arange(GRID_W)
    c0 = np.clip(c - NA_KW // 2, 0, GRID_W - NA_KW)
    col_ok = (c[None, :] >= c0[:, None]) & (c[None, :] < c0[:, None] + NA_KW)
    return slot, col_ok


def _na_bias(rpb, slot, col_ok):
    heads, n_dr, _ = rpb.shape
    rpb = rpb * LOG2E
    w = jnp.concatenate([rpb[..., NA_KW - 1:],
                         jnp.full((heads, n_dr, LANES - (2 * NA_KW - 1)), NEG, F32),
                         rpb[..., :NA_KW - 1]], axis=-1)
    t = jnp.tile(w, (1, 1, GRID_W))[..., :GRID_W * (LANES - 1)]
    t = t.reshape(heads, n_dr, GRID_W, LANES - 1)[..., :GRID_W]
    t = jnp.where(col_ok[None, None], t, NEG)
    t = jnp.concatenate([t, jnp.full((heads, 1, GRID_W, GRID_W), NEG, F32)], axis=1)
    n_pat, n_qr, n_kr = slot.shape
    blocks = jnp.stack([t[:, int(s)] for s in slot.reshape(-1)], axis=1)
    blocks = blocks.reshape(heads, n_pat, n_qr, n_kr, GRID_W, GRID_W)
    blocks = jnp.transpose(blocks, (1, 0, 2, 4, 3, 5))
    return blocks.reshape(n_pat, heads * n_qr * GRID_W, n_kr * GRID_W)


POOL_HALO = 8
POOL_ROWS = 256


def _pool_kernel(z_ref, g_ref, cnt_ref, w_ref, s_ref, o_ref, zp_ref, *, length):
    zeros = jnp.zeros((POOL_HALO, GROUP_W), F32)
    zp_ref[0:POOL_HALO, :] = zeros
    zp_ref[POOL_HALO + length:2 * POOL_HALO + length, :] = zeros
    zp_ref[POOL_HALO:POOL_HALO + length, :] = z_ref[0].astype(F32)
    first = lax.broadcasted_iota(jnp.int32, (POOL_ROWS, LANES), 1) < HEAD_DIM

    def window(t0, lanes, offsets):
        acc = None
        for o in offsets:
            v = zp_ref[POOL_HALO + t0 + o:POOL_HALO + t0 + o + POOL_ROWS, lanes]
            acc = v if acc is None else acc + v
        return acc

    for t0 in range(0, length, POOL_ROWS):
        lo, hi = slice(0, LANES), slice(LANES, 2 * LANES)
        s2 = window(t0, lo, (-1, 0))
        s4 = s2 + window(t0, lo, (-2, 1))
        s8 = window(t0, hi, range(-4, 4))
        s16 = s8 + window(t0, hi, tuple(range(-8, -4)) + tuple(range(4, 8)))
        win = jnp.concatenate([jnp.where(first, s2, s4), jnp.where(first, s8, s16)], axis=-1)
        rows = slice(t0, t0 + POOL_ROWS)
        z = zp_ref[POOL_HALO + t0:POOL_HALO + t0 + POOL_ROWS, :]
        pooled = win / cnt_ref[rows, :] - z
        y = _dot(pooled.astype(BF16), w_ref[...]) * s_ref[...]
        o_ref[0, rows, :] = (y * g_ref[0, rows, :].astype(F32)).astype(BF16)


def _pool(p, cnt, w_bd, scale):
    bsz, length, _ = p.shape
    return pl.pallas_call(
        functools.partial(_pool_kernel, length=length),
        grid=(bsz,),
        in_specs=[
            pl.BlockSpec((1, length, GROUP_W), lambda b: (b, 0, _col(P_BZ))),
            pl.BlockSpec((1, length, GROUP_W), lambda b: (b, 0, _col(P_GB))),
            pl.BlockSpec((length, GROUP_W), lambda b: (0, 0)),
            pl.BlockSpec((GROUP_W, GROUP_W), lambda b: (0, 0)),
            pl.BlockSpec((1, GROUP_W), lambda b: (0, 0)),
        ],
        out_specs=pl.BlockSpec((1, length, GROUP_W), lambda b: (b, 0, 0)),
        out_shape=jax.ShapeDtypeStruct((bsz, length, GROUP_W), BF16),
        scratch_shapes=[pltpu.VMEM((length + 2 * POOL_HALO, GROUP_W), F32)],
        compiler_params=pltpu.CompilerParams(vmem_limit_bytes=VMEM_LIMIT),
        name="pool_mixer",
    )(p, p, cnt, w_bd, scale)


def _pool_counts(length):
    t = np.arange(length)
    cols = []
    for w in POOL_WINDOWS:
        lo = np.maximum(t - w // 2, 0)
        hi = np.minimum(t + w // 2 - 1, length - 1)
        cols.append(np.repeat((hi - lo + 1).astype(np.float32)[:, None], HEAD_DIM, axis=1))
    return np.concatenate(cols, axis=1)


def _out_kernel(x_ref, gate_ref, a_ref, b_ref, n_ref, w_ref, o_ref):
    y = (_dot(a_ref[0], w_ref[0:512, :]) + _dot(b_ref[0], w_ref[512:768, :])
         + _dot(n_ref[0], w_ref[768:1024, :]))
    o_ref[0] = x_ref[0] + gate_ref[0] * y


def _output(x, mods, mod_row, a, b, n, w_out, tm):
    bsz, length, _ = x.shape
    tok = lambda width: pl.BlockSpec((1, tm, width), lambda bb, i: (bb, i, 0))
    return pl.pallas_call(
        _out_kernel,
        grid=(bsz, length // tm),
        in_specs=[
            tok(D_MODEL),
            pl.BlockSpec((1, 1, D_MODEL), lambda bb, i: (mod_row(bb), 0, 2)),
            tok(2 * GROUP_W), tok(GROUP_W), tok(GROUP_W),
            pl.BlockSpec((D_MODEL, D_MODEL), lambda bb, i: (0, 0)),
        ],
        out_specs=tok(D_MODEL),
        out_shape=jax.ShapeDtypeStruct(x.shape, F32),
        compiler_params=pltpu.CompilerParams(vmem_limit_bytes=VMEM_LIMIT),
        name="output_projection",
    )(x, mods, a, b, n, w_out)


def _rope_tables(seq):
    t = jnp.arange(seq)
    half = HEAD_DIM // 4
    inv_freq = ROPE_THETA ** (-jnp.arange(half, dtype=jnp.float32) / half)
    cos, sa, sb = [], [], []
    zero = jnp.zeros((seq, half), F32)
    for pos in (t // GRID_W, t % GRID_W):
        ang = pos.astype(jnp.float32)[:, None] * inv_freq[None, :]
        cos += [jnp.cos(ang), jnp.cos(ang)]
        sa += [-jnp.sin(ang), zero]
        sb += [zero, jnp.sin(ang)]
    tile = lambda parts: jnp.tile(jnp.concatenate(parts, axis=-1), (1, LANES // HEAD_DIM))
    return tile(cos), tile(sa), tile(sb)


def _layout_constants():
    lane = np.arange(GROUP_W)
    seg = lane[:, None] // HEAD_DIM == lane[None, :] // HEAD_DIM
    src = np.arange(LANES)[:, None]
    col = np.arange(2 * GROUP_W)[None, :]
    rep_k = src == (col // GROUP_W) * HEAD_DIM + col % HEAD_DIM
    colv = np.arange(2 * LANES)[None, :]
    is_v = colv % LANES < HEAD_DIM
    rep_v = is_v & (src == (colv // LANES) * HEAD_DIM + colv % LANES)
    return (jnp.asarray(seg, BF16), jnp.asarray(rep_k, BF16), jnp.asarray(rep_v, BF16),
            jnp.asarray(~is_v, F32))


def kernel(x, c, ctx, c_ctx, norm_gain, w_mod, b_mod, w_in, att_q_gain, att_k_gain,
           pool_w, pool_scale, na_q_gain, na_k_gain, na_rpb, w_out):
    bsz, seq, _ = x.shape
    n_ctx = ctx.shape[1]
    assert seq == NA_GROUPS * NA_TQ and n_ctx % KV_CHUNK == 0 and bsz < MOD_ROWS
    assert bsz % CTX_NB == 0 and bsz % NA_NB == 0

    mods_all = _modulation(c, c_ctx, w_mod, b_mod)
    tables = _rope_tables(seq)
    tables_ctx = (jnp.ones((n_ctx, LANES), F32), jnp.zeros((n_ctx, LANES), F32),
                  jnp.zeros((n_ctx, LANES), F32))
    consts = _layout_constants()
    cnt, cnt_ctx = jnp.asarray(_pool_counts(seq)), jnp.asarray(_pool_counts(n_ctx))
    slot, col_ok = _na_bias_layout()
    w_in_b = w_in.astype(BF16)
    w_out_b = w_out.astype(BF16)
    wide = lambda g: jnp.tile(g, GROUP_W // HEAD_DIM)[None, :]
    lat_row = lambda b: b
    ctx_row = lambda b: bsz

    for l in range(DEPTH):
        last = l == DEPTH - 1
        mods = mods_all[l].reshape(MOD_ROWS, 1, 3 * D_MODEL)
        gains = (wide(att_q_gain[l]), wide(att_k_gain[l]), wide(na_q_gain[l]),
                 wide(na_k_gain[l]))
        ng = norm_gain[l][None, :]
        p = _projection(x, mods, lat_row, ng, w_in_b[l], tables, gains, consts, TOKEN_TILE)
        p_ctx = _projection(ctx, mods, ctx_row, ng, w_in_b[l], tables_ctx, gains, consts, 256)
        bias = _na_bias(na_rpb[l], slot, col_ok)
        w_bd = jax.scipy.linalg.block_diag(*[pool_w[l, g] for g in range(4)]).astype(BF16)
        ps = pool_scale[l][None, :]

        peak = lambda g: jnp.max(jnp.abs(g[l]))
        use_fp8 = jnp.maximum(peak(att_q_gain) * peak(att_k_gain),
                              peak(na_q_gain) * peak(na_k_gain)) <= FP8_GAIN_LIMIT

        def mixers(fp8):
            outs = (_gqa_latent(p, p_ctx, fp8), _na_latent(p, p_ctx, bias, fp8))
            if not last:
                outs += (_gqa_context(p_ctx, fp8), _na_context(p_ctx, fp8))
            return outs

        mixed = lax.cond(use_fp8, lambda: mixers(True), lambda: mixers(False))
        a, n = mixed[:2]
        bo = _pool(p, cnt, w_bd, ps)
        x_new = _output(x, mods, lat_row, a, bo, n, w_out_b[l], OUT_TILE)

        if not last:
            a_c, n_c = mixed[2:]
            bo_c = _pool(p_ctx, cnt_ctx, w_bd, ps)
            ctx = _output(ctx, mods, ctx_row, a_c, bo_c, n_c, w_out_b[l], 256)
        x = x_new
    return x
```

```python
import functools
import math

import numpy as np
import jax
import jax.numpy as jnp
from jax import lax
from jax.experimental import pallas as pl
from jax.experimental.pallas import tpu as pltpu

F32 = jnp.float32
BF16 = jnp.bfloat16
F8 = jnp.float8_e4m3fn
P_SHIFT = 8.0
F8_MAX = 448.0
FP8_GAIN_LIMIT = 2.0
FP8_PEAK_GAIN = 0.98 * F8_MAX / (8.0 * 2.0 ** 0.5)

D_MODEL = 1024
DEPTH = 4
GRID_W = 64
HEAD_DIM = 64
ROPE_THETA = 10000.0
EPS = 1e-6
ATTN_SCALE = HEAD_DIM ** -0.5
LOG2E = 1.4426950408889634
Q_SCALE = ATTN_SCALE * LOG2E
POOL_WINDOWS = (2, 4, 8, 16)
NA_KH = 8
NA_KW = 16
IN_WIDTH = 2816

LANES = 128
GROUP_W = 256
HEADS_PER_GROUP = GROUP_W // HEAD_DIM
KV_CHUNK = 256
TILES_PER_ITER = 16

W_AQ, W_AK, W_AV, W_AG, W_BZ, W_BG, W_NQ, W_NK, W_NV, W_NG = (
    0, 512, 640, 768, 1280, 1536, 1792, 2048, 2304, 2560)
P_QA, P_KE, P_VE, P_GA, P_BZ, P_GB, P_NQ, P_NK, P_NV, P_GN = (
    0, 512, 1024, 1280, 1792, 2048, 2304, 2560, 2816, 3072)
P_WIDTH = 3328
MOD_ROWS = 24
TOKEN_TILE = 512
OUT_TILE = 1024
NEG = -1e30
VMEM_LIMIT = 48 * 1024 * 1024


def _silu(v):
    return v / (1.0 + jnp.exp(-v))


def _split_bf16(v):
    hi = v.astype(BF16)
    lo = (v - hi.astype(F32)).astype(BF16)
    return hi, lo


def _dot(a, b):
    return jnp.dot(a, b, preferred_element_type=F32)


def _mod_kernel(c_ref, w_ref, b_ref, o_ref):
    a_hi, a_lo = _split_bf16(_silu(c_ref[...]))
    w_hi, w_lo = _split_bf16(w_ref[0])
    o_ref[0] = _dot(a_hi, w_hi) + _dot(a_lo, w_hi) + _dot(a_hi, w_lo) + b_ref[0]


def _modulation(c, c_ctx, w_mod, b_mod):
    bsz = c.shape[0]
    cs = jnp.concatenate(
        [c, c_ctx[None], jnp.zeros((MOD_ROWS - bsz - 1, D_MODEL), F32)], axis=0)
    return pl.pallas_call(
        _mod_kernel,
        grid=(DEPTH, 3),
        in_specs=[
            pl.BlockSpec((MOD_ROWS, D_MODEL), lambda l, j: (0, 0)),
            pl.BlockSpec((1, D_MODEL, D_MODEL), lambda l, j: (l, 0, j)),
            pl.BlockSpec((1, 1, D_MODEL), lambda l, j: (l, 0, j)),
        ],
        out_specs=pl.BlockSpec((1, MOD_ROWS, D_MODEL), lambda l, j: (l, 0, j)),
        out_shape=jax.ShapeDtypeStruct((DEPTH, MOD_ROWS, 3 * D_MODEL), F32),
        compiler_params=pltpu.CompilerParams(vmem_limit_bytes=VMEM_LIMIT),
        name="modulation",
    )(cs, w_mod, b_mod.reshape(DEPTH, 1, 3 * D_MODEL))


def _proj_kernel(x_ref, shift_ref, scale_ref, ng_ref, w_ref, cos_ref, sa_ref, sb_ref,
                 aqg_ref, akg_ref, nqg_ref, nkg_ref, seg_ref, repk_ref, repv_ref, ones_ref,
                 o_ref, hb0_ref, hb1_ref, y0_ref, y1_ref):
    step = pl.program_id(0)

    @pl.when(step == 0)
    def _():
        hb1_ref[...] = jnp.zeros_like(hb1_ref)
        y0_ref[...] = jnp.zeros_like(y0_ref)
        y1_ref[...] = jnp.zeros_like(y1_ref)

    def pre_norm(hb_ref):
        x = x_ref[0]
        ms = jnp.mean(x * x, axis=-1, keepdims=True)
        h = x * lax.rsqrt(ms + EPS) * ng_ref[...]
        h = h * (1.0 + scale_ref[0]) + shift_ref[0]
        hb_ref[...] = h.astype(BF16)

    def stage(hb_w, hb_r, y_w, y_r):
        pre_norm(hb_w)
        epilogue = _proj_epilogue(y_r, cos_ref, sa_ref, sb_ref, aqg_ref, akg_ref, nqg_ref,
                                  nkg_ref, seg_ref, repk_ref, repv_ref, ones_ref, o_ref)
        for c0 in range(0, IN_WIDTH, GROUP_W):
            y_w[:, c0:c0 + GROUP_W] = _dot(hb_r[...], w_ref[:, c0:c0 + GROUP_W])
            next(epilogue)

    @pl.when(step % 2 == 0)
    def _():
        stage(hb0_ref, hb1_ref, y1_ref, y0_ref)

    @pl.when(step % 2 == 1)
    def _():
        stage(hb1_ref, hb0_ref, y0_ref, y1_ref)


def _proj_epilogue(y_ref, cos_ref, sa_ref, sb_ref, aqg_ref, akg_ref, nqg_ref, nkg_ref,
                   seg_ref, repk_ref, repv_ref, ones_ref, o_ref):
    def proj(c0, width):
        return y_ref[:, c0:c0 + width]

    def head_norm(y, g):
        ss = _dot((y * y).astype(BF16), seg_ref[...])
        return y * lax.rsqrt(ss * (1.0 / HEAD_DIM) + EPS) * g

    def rope(y):
        width = y.shape[-1]
        wide = lambda t: jnp.tile(t[...], (1, width // LANES))
        return (y * wide(cos_ref) + pltpu.roll(y, width - 16, 1) * wide(sa_ref)
                + pltpu.roll(y, 16, 1) * wide(sb_ref))

    def put(c0, v):
        o_ref[0, :, c0:c0 + v.shape[-1]] = v.astype(BF16)

    for j in range(2):
        y = proj(W_AQ + j * GROUP_W, GROUP_W)
        put(P_QA + j * GROUP_W, rope(head_norm(y, aqg_ref[...])) * Q_SCALE)
        yield
    kv = proj(W_AK, GROUP_W)
    kb = rope(head_norm(kv, akg_ref[...]))[:, :LANES].astype(BF16)
    put(P_KE, _dot(kb, repk_ref[...]))
    put(P_VE, _dot(kv[:, LANES:].astype(BF16), repv_ref[...]) + ones_ref[...])
    yield
    for j in range(2):
        put(P_GA + j * GROUP_W, _silu(proj(W_AG + j * GROUP_W, GROUP_W)))
        yield
    put(P_BZ, proj(W_BZ, 256))
    yield
    put(P_GB, _silu(proj(W_BG, 256)))
    yield
    put(P_NQ, head_norm(proj(W_NQ, GROUP_W), nqg_ref[...]) * Q_SCALE)
    yield
    put(P_NK, head_norm(proj(W_NK, GROUP_W), nkg_ref[...]))
    yield
    put(P_NV, proj(W_NV, 256))
    yield
    put(P_GN, _silu(proj(W_NG, 256)))
    yield


def _projection(x, mods, mod_row, norm_gain, w_in, tables, gains, consts, tm):
    bsz, length, _ = x.shape
    cos, sa, sb = tables
    per_seq = length // tm
    n_tiles = bsz * per_seq
    norm_tile = lambda i: jnp.minimum(i, n_tiles - 1)
    out_tile = lambda i: jnp.maximum(i - 2, 0)
    vec = lambda k: pl.BlockSpec(
        (1, 1, D_MODEL), lambda i: (mod_row(norm_tile(i) // per_seq), 0, k))
    const2 = lambda a: pl.BlockSpec(a.shape, lambda i: (0, 0))
    tab = pl.BlockSpec((tm, LANES), lambda i: (out_tile(i) % per_seq, 0))
    hidden = pltpu.VMEM((tm, D_MODEL), BF16)
    projected = pltpu.VMEM((tm, IN_WIDTH), F32)
    return pl.pallas_call(
        _proj_kernel,
        grid=(n_tiles + 2,),
        in_specs=[
            pl.BlockSpec((1, tm, D_MODEL),
                         lambda i: (norm_tile(i) // per_seq, norm_tile(i) % per_seq, 0)),
            vec(0), vec(1), const2(norm_gain), const2(w_in), tab, tab, tab,
            *[const2(g) for g in gains], *[const2(a) for a in consts],
        ],
        out_specs=pl.BlockSpec((1, tm, P_WIDTH),
                               lambda i: (out_tile(i) // per_seq, out_tile(i) % per_seq, 0)),
        out_shape=jax.ShapeDtypeStruct((bsz, length, P_WIDTH), BF16),
        scratch_shapes=[hidden, hidden, projected, projected],
        compiler_params=pltpu.CompilerParams(
            vmem_limit_bytes=VMEM_LIMIT, dimension_semantics=("arbitrary",)),
        name="projection",
    )(x, mods, mods, norm_gain, w_in, cos, sa, sb, *gains, *consts)


def _attn_kernel(*refs, tq, tiles_per_row, n_tiles, tiles_per_iter, n_kv, chunks, n_bias,
                 sums_in_v, fp8):
    q_ref, g_ref = refs[0], refs[1]
    kv = refs[2:2 + 2 * n_kv]
    pos = 2 + 2 * n_kv
    bias_ref = refs[pos] if n_bias else None
    pos += 1 if n_bias else 0
    o_ref, s_a, s_b, m_a, m_b, mc_ref, lc_ref = refs[pos:pos + 7]
    mm_dtype = BF16
    if fp8:
        mm_dtype = F8
        kv8 = refs[pos + 7:pos + 7 + 2 * n_kv]
        for src, dst in zip(kv, kv8):
            dst[...] = src[...].astype(F8)
        kv = kv8
    rows = HEADS_PER_GROUP * tq
    n_chunks = len(chunks)
    lane_head = jnp.right_shift(lax.broadcasted_iota(jnp.int32, (tq, GROUP_W), 1), 6)
    low_half = lax.broadcasted_iota(jnp.int32, (tq, LANES), 1) < HEAD_DIM
    shift = tiles_per_row.bit_length() - 1

    def tile_pos(t):
        if tiles_per_row == 1:
            return t, 0
        r0 = (t & (tiles_per_row - 1)) * tq
        return t >> shift, (r0 if isinstance(r0, int) else pl.multiple_of(r0, tq))

    def stage(t1, s1, m1, t0, s0, m0):
        if t1 is not None:
            b1, r1 = tile_pos(t1)
            q = q_ref[b1, pl.ds(r1, tq), :].astype(F32)
            qs = jnp.concatenate([jnp.where(lane_head == h, q, 0.0).astype(mm_dtype)
                                  for h in range(HEADS_PER_GROUP)], axis=0)
        if t0 is not None:
            b0, r0 = tile_pos(t0)
            m_prev = m0[...]
            m_prev = jnp.concatenate([m_prev, m_prev], axis=-1)
            acc = None
        for ci, (ai, off) in enumerate(chunks):
            if t1 is not None:
                k = kv[2 * ai][b1, off:off + KV_CHUNK, :]
                s = lax.dot_general(qs, k, (((1,), (1,)), ((), ())),
                                    preferred_element_type=F32)
                if ci < n_bias:
                    s = s + bias_ref[0, :, ci * KV_CHUNK:(ci + 1) * KV_CHUNK]
                s1[ci] = s
                mc_ref[ci] = jnp.maximum(s[:, :LANES], s[:, LANES:])
            if t0 is not None:
                z = s0[ci] - m_prev
                p = jnp.exp2(z.astype(BF16) if fp8 else z)
                if not sums_in_v:
                    lc_ref[ci] = (p[:, :LANES] + p[:, LANES:]).astype(F32)
                d = _dot(p.astype(mm_dtype), kv[2 * ai + 1][b0, off:off + KV_CHUNK, :])
                acc = d if acc is None else acc + d
        if t1 is not None:
            m = mc_ref[0]
            for ci in range(1, n_chunks):
                m = jnp.maximum(m, mc_ref[ci])
            m = jnp.max(m, axis=-1, keepdims=True) - (P_SHIFT if fp8 else 0.0)
            m1[...] = jnp.broadcast_to(m, (rows, LANES))
        if t0 is not None:
            gate = g_ref[b0, pl.ds(r0, tq), :].astype(F32)
            head = lambda a, h: a[h * tq:(h + 1) * tq, :]
            if sums_in_v:
                o = acc / pltpu.roll(acc, HEAD_DIM, 1)
                halves = [jnp.where(low_half, head(o, h), pltpu.roll(head(o, h + 1), HEAD_DIM, 1))
                          for h in (0, 2)]
                out = jnp.concatenate(halves, axis=-1)
            else:
                l = lc_ref[0]
                for ci in range(1, n_chunks):
                    l = l + lc_ref[ci]
                o = acc / jnp.sum(l, axis=-1, keepdims=True)
                out = jnp.zeros((tq, GROUP_W), F32)
                for h in range(HEADS_PER_GROUP):
                    out = out + jnp.where(lane_head == h, head(o, h), 0.0)
            o_ref[b0, pl.ds(r0, tq), :] = (out * gate).astype(BF16)

    stage(0, s_a, m_a, None, None, None)

    if tiles_per_iter == n_tiles:
        for t in range(0, n_tiles, 2):
            stage(t + 1, s_b, m_b, t, s_a, m_a)
            stage(t + 2 if t + 2 < n_tiles else None, s_a, m_a, t + 1, s_b, m_b)
        return

    def body(j, carry):
        t = tiles_per_iter * j
        for u in range(0, tiles_per_iter, 2):
            stage(t + u + 1, s_b, m_b, t + u, s_a, m_a)
            stage(jnp.minimum(t + u + 2, n_tiles - 1), s_a, m_a, t + u + 1, s_b, m_b)
        return carry

    lax.fori_loop(0, n_tiles // tiles_per_iter, body, 0)


def _attention(grid, nb, q_rows, q, gate, kvs, bias, out_shape, out_map, tq, chunks, name,
               fp8=False):
    n_kv = len(kvs) // 2
    n_bias = 0
    operands = [q[0], gate[0]]
    specs = [pl.BlockSpec((nb, q_rows, GROUP_W), q[1]),
             pl.BlockSpec((nb, q_rows, GROUP_W), gate[1])]
    for arr, kv_rows, kv_lanes, imap in kvs:
        operands.append(arr)
        specs.append(pl.BlockSpec((nb, kv_rows, kv_lanes), imap))
    sums_in_v = kvs[1][2] == LANES
    if bias is not None:
        arr, imap = bias
        n_bias = arr.shape[-1] // KV_CHUNK
        operands.append(arr)
        specs.append(pl.BlockSpec((1,) + arr.shape[1:], imap))
    rows = HEADS_PER_GROUP * tq
    tiles_per_row = q_rows // tq
    n_tiles = nb * tiles_per_row
    assert n_tiles % 2 == 0 and tiles_per_row & (tiles_per_row - 1) == 0
    score_buf = pltpu.VMEM((len(chunks), rows, KV_CHUNK), F32)
    max_buf = pltpu.VMEM((rows, LANES), F32)
    part_buf = pltpu.VMEM((len(chunks), rows, LANES), F32)
    sum_buf = pltpu.VMEM((1, 8, LANES) if sums_in_v else (len(chunks), rows, LANES), F32)
    kv8 = [pltpu.VMEM((nb, kv_rows, kv_lanes), F8) for _, kv_rows, kv_lanes, _ in kvs] if fp8 else []
    return pl.pallas_call(
        functools.partial(_attn_kernel, tq=tq, tiles_per_row=tiles_per_row, n_tiles=n_tiles,
                          tiles_per_iter=math.gcd(n_tiles, TILES_PER_ITER), n_kv=n_kv,
                          chunks=chunks, n_bias=n_bias, sums_in_v=sums_in_v, fp8=fp8),
        grid=grid,
        in_specs=specs,
        out_specs=pl.BlockSpec((nb, q_rows, GROUP_W), out_map),
        out_shape=jax.ShapeDtypeStruct(out_shape, BF16),
        scratch_shapes=[score_buf, score_buf, max_buf, max_buf, part_buf, sum_buf, *kv8],
        compiler_params=pltpu.CompilerParams(vmem_limit_bytes=VMEM_LIMIT),
        name=name,
    )(*operands)


def _col(off):
    return off // GROUP_W


GQA_TQ = 128
CTX_NB = 8


def _gqa_kv(p):
    rows = p.shape[1]
    return [(p, rows, GROUP_W, lambda b, j: (b, 0, _col(P_KE) + j)),
            (p, rows, LANES, lambda b, j: (b, 0, P_VE // LANES + j))]


def _gqa_latent(p, p_ctx, fp8):
    bsz, length, _ = p.shape
    n_ctx = p_ctx.shape[1]
    kvs = _gqa_kv(p) + _gqa_kv(p_ctx)
    chunks = tuple((0, c * KV_CHUNK) for c in range(length // KV_CHUNK)) + tuple(
        (1, c * KV_CHUNK) for c in range(n_ctx // KV_CHUNK))
    return _attention(
        grid=(bsz, 2), nb=1, q_rows=length,
        q=(p, lambda b, j: (b, 0, _col(P_QA) + j)),
        gate=(p, lambda b, j: (b, 0, _col(P_GA) + j)),
        kvs=kvs, bias=None,
        out_shape=(bsz, length, 2 * GROUP_W), out_map=lambda b, j: (b, 0, j),
        tq=GQA_TQ, chunks=chunks, name="gqa_attention", fp8=fp8)


def _gqa_context(p_ctx, fp8):
    bsz, n_ctx, _ = p_ctx.shape
    return _attention(
        grid=(bsz // CTX_NB, 2), nb=CTX_NB, q_rows=n_ctx,
        q=(p_ctx, lambda b, j: (b, 0, _col(P_QA) + j)),
        gate=(p_ctx, lambda b, j: (b, 0, _col(P_GA) + j)),
        kvs=_gqa_kv(p_ctx), bias=None,
        out_shape=(bsz, n_ctx, 2 * GROUP_W), out_map=lambda b, j: (b, 0, j),
        tq=GQA_TQ, chunks=tuple((0, c * KV_CHUNK) for c in range(n_ctx // KV_CHUNK)),
        name="gqa_context_attention", fp8=fp8)


NA_TQ = 256
NA_BAND = 3
NA_GROUPS = 2048 // NA_TQ
NA_NB = 4
NA_PATTERN_GROUPS = (0, 1, NA_GROUPS - 1)


def _na_band_start(g):
    return jnp.clip(g - 1, 0, NA_GROUPS - NA_BAND)


def _na_latent(p, p_ctx, bias, fp8):
    bsz, length, _ = p.shape
    kvs = []
    for i in range(NA_BAND):
        kvs.append((p, KV_CHUNK, GROUP_W,
                    lambda g, b, i=i: (b, _na_band_start(g) + i, _col(P_NK))))
        kvs.append((p, KV_CHUNK, GROUP_W,
                    lambda g, b, i=i: (b, _na_band_start(g) + i, _col(P_NV))))
    n_ctx = p_ctx.shape[1]
    kvs += [(p_ctx, n_ctx, GROUP_W, lambda g, b: (b, 0, _col(P_NK))),
            (p_ctx, n_ctx, GROUP_W, lambda g, b: (b, 0, _col(P_NV)))]
    chunks = tuple((i, 0) for i in range(NA_BAND)) + tuple(
        (NA_BAND, c * KV_CHUNK) for c in range(n_ctx // KV_CHUNK))
    pattern = lambda g, b: (jnp.where(g == 0, 0, jnp.where(g == NA_GROUPS - 1, 2, 1)), 0, 0)
    return _attention(
        grid=(NA_GROUPS, bsz // NA_NB), nb=NA_NB, q_rows=NA_TQ,
        q=(p, lambda g, b: (b, g, _col(P_NQ))),
        gate=(p, lambda g, b: (b, g, _col(P_GN))),
        kvs=kvs, bias=(bias, pattern),
        out_shape=(bsz, length, GROUP_W), out_map=lambda g, b: (b, g, 0),
        tq=NA_TQ, chunks=chunks, name="na_attention", fp8=fp8)


def _na_context(p_ctx, fp8):
    bsz, n_ctx, _ = p_ctx.shape
    kvs = [(p_ctx, n_ctx, GROUP_W, lambda b: (b, 0, _col(P_NK))),
           (p_ctx, n_ctx, GROUP_W, lambda b: (b, 0, _col(P_NV)))]
    return _attention(
        grid=(bsz // CTX_NB,), nb=CTX_NB, q_rows=n_ctx,
        q=(p_ctx, lambda b: (b, 0, _col(P_NQ))),
        gate=(p_ctx, lambda b: (b, 0, _col(P_GN))),
        kvs=kvs, bias=None,
        out_shape=(bsz, n_ctx, GROUP_W), out_map=lambda b: (b, 0, 0),
        tq=NA_TQ, chunks=tuple((0, c * KV_CHUNK) for c in range(n_ctx // KV_CHUNK)),
        name="na_context_attention", fp8=fp8)


def _na_bias_layout():
    rows = 2048 // GRID_W
    rows_per_tile, rows_per_chunk = NA_TQ // GRID_W, KV_CHUNK // GRID_W
    band_rows = NA_BAND * rows_per_chunk
    slot = np.full((len(NA_PATTERN_GROUPS), rows_per_tile, band_rows), 2 * NA_KH - 1)
    for pat, g in enumerate(NA_PATTERN_GROUPS):
        band0 = rows_per_chunk * min(max(g - 1, 0), NA_GROUPS - NA_BAND)
        for qr in range(rows_per_tile):
            r = rows_per_tile * g + qr
            r0 = min(max(r - NA_KH // 2, 0), rows - NA_KH)
            for kr in range(band_rows):
                rp = band0 + kr
                if r0 <= rp < r0 + NA_KH:
                    slot[pat, qr, kr] = rp - r + NA_KH - 1
    c = np.arange(GRID_W)
    c0 = np.clip(c - NA_KW // 2, 0, GRID_W - NA_KW)
    col_ok = (c[None, :] >= c0[:, None]) & (c[None, :] < c0[:, None] + NA_KW)
    return slot, col_ok


def _na_bias(rpb, slot, col_ok):
    heads, n_dr, _ = rpb.shape
    rpb = rpb * LOG2E
    w = jnp.concatenate([rpb[..., NA_KW - 1:],
                         jnp.full((heads, n_dr, LANES - (2 * NA_KW - 1)), NEG, F32),
                         rpb[..., :NA_KW - 1]], axis=-1)
    t = jnp.tile(w, (1, 1, GRID_W))[..., :GRID_W * (LANES - 1)]
    t = t.reshape(heads, n_dr, GRID_W, LANES - 1)[..., :GRID_W]
    t = jnp.where(col_ok[None, None], t, NEG)
    t = jnp.concatenate([t, jnp.full((heads, 1, GRID_W, GRID_W), NEG, F32)], axis=1)
    n_pat, n_qr, n_kr = slot.shape
    blocks = jnp.stack([t[:, int(s)] for s in slot.reshape(-1)], axis=1)
    blocks = blocks.reshape(heads, n_pat, n_qr, n_kr, GRID_W, GRID_W)
    blocks = jnp.transpose(blocks, (1, 0, 2, 4, 3, 5))
    return blocks.reshape(n_pat, heads * n_qr * GRID_W, n_kr * GRID_W)


POOL_HALO = 8
POOL_ROWS = 256


def _pool_kernel(z_ref, g_ref, cnt_ref, w_ref, s_ref, o_ref, zp_ref, *, length):
    zeros = jnp.zeros((POOL_HALO, GROUP_W), F32)
    zp_ref[0:POOL_HALO, :] = zeros
    zp_ref[POOL_HALO + length:2 * POOL_HALO + length, :] = zeros
    zp_ref[POOL_HALO:POOL_HALO + length, :] = z_ref[0].astype(F32)
    first = lax.broadcasted_iota(jnp.int32, (POOL_ROWS, LANES), 1) < HEAD_DIM

    def window(t0, lanes, offsets):
        acc = None
        for o in offsets:
            v = zp_ref[POOL_HALO + t0 + o:POOL_HALO + t0 + o + POOL_ROWS, lanes]
            acc = v if acc is None else acc + v
        return acc

    for t0 in range(0, length, POOL_ROWS):
        lo, hi = slice(0, LANES), slice(LANES, 2 * LANES)
        s2 = window(t0, lo, (-1, 0))
        s4 = s2 + window(t0, lo, (-2, 1))
        s8 = window(t0, hi, range(-4, 4))
        s16 = s8 + window(t0, hi, tuple(range(-8, -4)) + tuple(range(4, 8)))
        win = jnp.concatenate([jnp.where(first, s2, s4), jnp.where(first, s8, s16)], axis=-1)
        rows = slice(t0, t0 + POOL_ROWS)
        z = zp_ref[POOL_HALO + t0:POOL_HALO + t0 + POOL_ROWS, :]
        pooled = win / cnt_ref[rows, :] - z
        y = _dot(pooled.astype(BF16), w_ref[...]) * s_ref[...]
        o_ref[0, rows, :] = (y * g_ref[0, rows, :].astype(F32)).astype(BF16)


def _pool(p, cnt, w_bd, scale):
    bsz, length, _ = p.shape
    return pl.pallas_call(
        functools.partial(_pool_kernel, length=length),
        grid=(bsz,),
        in_specs=[
            pl.BlockSpec((1, length, GROUP_W), lambda b: (b, 0, _col(P_BZ))),
            pl.BlockSpec((1, length, GROUP_W), lambda b: (b, 0, _col(P_GB))),
            pl.BlockSpec((length, GROUP_W), lambda b: (0, 0)),
            pl.BlockSpec((GROUP_W, GROUP_W), lambda b: (0, 0)),
            pl.BlockSpec((1, GROUP_W), lambda b: (0, 0)),
        ],
        out_specs=pl.BlockSpec((1, length, GROUP_W), lambda b: (b, 0, 0)),
        out_shape=jax.ShapeDtypeStruct((bsz, length, GROUP_W), BF16),
        scratch_shapes=[pltpu.VMEM((length + 2 * POOL_HALO, GROUP_W), F32)],
        compiler_params=pltpu.CompilerParams(vmem_limit_bytes=VMEM_LIMIT),
        name="pool_mixer",
    )(p, p, cnt, w_bd, scale)


def _pool_counts(length):
    t = np.arange(length)
    cols = []
    for w in POOL_WINDOWS:
        lo = np.maximum(t - w // 2, 0)
        hi = np.minimum(t + w // 2 - 1, length - 1)
        cols.append(np.repeat((hi - lo + 1).astype(np.float32)[:, None], HEAD_DIM, axis=1))
    return np.concatenate(cols, axis=1)


def _out_kernel(x_ref, gate_ref, a_ref, b_ref, n_ref, w_ref, o_ref):
    y = (_dot(a_ref[0], w_ref[0:512, :]) + _dot(b_ref[0], w_ref[512:768, :])
         + _dot(n_ref[0], w_ref[768:1024, :]))
    o_ref[0] = x_ref[0] + gate_ref[0] * y


def _output(x, mods, mod_row, a, b, n, w_out, tm):
    bsz, length, _ = x.shape
    tok = lambda width: pl.BlockSpec((1, tm, width), lambda bb, i: (bb, i, 0))
    return pl.pallas_call(
        _out_kernel,
        grid=(bsz, length // tm),
        in_specs=[
            tok(D_MODEL),
            pl.BlockSpec((1, 1, D_MODEL), lambda bb, i: (mod_row(bb), 0, 2)),
            tok(2 * GROUP_W), tok(GROUP_W), tok(GROUP_W),
            pl.BlockSpec((D_MODEL, D_MODEL), lambda bb, i: (0, 0)),
        ],
        out_specs=tok(D_MODEL),
        out_shape=jax.ShapeDtypeStruct(x.shape, F32),
        compiler_params=pltpu.CompilerParams(vmem_limit_bytes=VMEM_LIMIT),
        name="output_projection",
    )(x, mods, a, b, n, w_out)


def _rope_tables(seq):
    t = jnp.arange(seq)
    half = HEAD_DIM // 4
    inv_freq = ROPE_THETA ** (-jnp.arange(half, dtype=jnp.float32) / half)
    cos, sa, sb = [], [], []
    zero = jnp.zeros((seq, half), F32)
    for pos in (t // GRID_W, t % GRID_W):
        ang = pos.astype(jnp.float32)[:, None] * inv_freq[None, :]
        cos += [jnp.cos(ang), jnp.cos(ang)]
        sa += [-jnp.sin(ang), zero]
        sb += [zero, jnp.sin(ang)]
    tile = lambda parts: jnp.tile(jnp.concatenate(parts, axis=-1), (1, LANES // HEAD_DIM))
    return tile(cos), tile(sa), tile(sb)


def _layout_constants():
    lane = np.arange(GROUP_W)
    seg = lane[:, None] // HEAD_DIM == lane[None, :] // HEAD_DIM
    src = np.arange(LANES)[:, None]
    col = np.arange(2 * GROUP_W)[None, :]
    rep_k = src == (col // GROUP_W) * HEAD_DIM + col % HEAD_DIM
    colv = np.arange(2 * LANES)[None, :]
    is_v = colv % LANES < HEAD_DIM
    rep_v = is_v & (src == (colv // LANES) * HEAD_DIM + colv % LANES)
    return (jnp.asarray(seg, BF16), jnp.asarray(rep_k, BF16), jnp.asarray(rep_v, BF16),
            jnp.asarray(~is_v, F32))


def kernel(x, c, ctx, c_ctx, norm_gain, w_mod, b_mod, w_in, att_q_gain, att_k_gain,
           pool_w, pool_scale, na_q_gain, na_k_gain, na_rpb, w_out):
    bsz, seq, _ = x.shape
    n_ctx = ctx.shape[1]
    assert seq == NA_GROUPS * NA_TQ and n_ctx % KV_CHUNK == 0 and bsz < MOD_ROWS
    assert bsz % CTX_NB == 0 and bsz % NA_NB == 0

    mods_all = _modulation(c, c_ctx, w_mod, b_mod)
    tables = _rope_tables(seq)
    tables_ctx = (jnp.ones((n_ctx, LANES), F32), jnp.zeros((n_ctx, LANES), F32),
                  jnp.zeros((n_ctx, LANES), F32))
    consts = _layout_constants()
    cnt, cnt_ctx = jnp.asarray(_pool_counts(seq)), jnp.asarray(_pool_counts(n_ctx))
    slot, col_ok = _na_bias_layout()
    w_in_b = w_in.astype(BF16)
    w_out_b = w_out.astype(BF16)
    wide = lambda g: jnp.tile(g, GROUP_W // HEAD_DIM)[None, :]
    lat_row = lambda b: b
    ctx_row = lambda b: bsz

    for l in range(DEPTH):
        last = l == DEPTH - 1
        mods = mods_all[l].reshape(MOD_ROWS, 1, 3 * D_MODEL)
        gains = (wide(att_q_gain[l]), wide(att_k_gain[l]), wide(na_q_gain[l]),
                 wide(na_k_gain[l]))
        ng = norm_gain[l][None, :]
        p = _projection(x, mods, lat_row, ng, w_in_b[l], tables, gains, consts, TOKEN_TILE)
        p_ctx = _projection(ctx, mods, ctx_row, ng, w_in_b[l], tables_ctx, gains, consts, 256)
        bias = _na_bias(na_rpb[l], slot, col_ok)
        w_bd = jax.scipy.linalg.block_diag(*[pool_w[l, g] for g in range(4)]).astype(BF16)
        ps = pool_scale[l][None, :]

        peak = lambda g: jnp.max(jnp.abs(g[l]))
        gains_ok = jnp.maximum(peak(att_q_gain) * peak(att_k_gain),
                               peak(na_q_gain) * peak(na_k_gain)) <= FP8_GAIN_LIMIT
        for g in (att_q_gain, att_k_gain, na_q_gain, na_k_gain):
            gains_ok &= peak(g) <= FP8_PEAK_GAIN
        value_peak = functools.reduce(jnp.maximum, [
            jnp.max(jnp.abs(arr[:, :, c0:c0 + GROUP_W]))
            for arr in (p, p_ctx) for c0 in (P_VE, P_NV)])
        use_fp8 = gains_ok & (value_peak <= F8_MAX)

        def mixers(fp8):
            outs = (_gqa_latent(p, p_ctx, fp8), _na_latent(p, p_ctx, bias, fp8))
            if not last:
                outs += (_gqa_context(p_ctx, fp8), _na_context(p_ctx, fp8))
            return outs

        mixed = lax.cond(use_fp8, lambda: mixers(True), lambda: mixers(False))
        a, n = mixed[:2]
        bo = _pool(p, cnt, w_bd, ps)
        x_new = _output(x, mods, lat_row, a, bo, n, w_out_b[l], OUT_TILE)

        if not last:
            a_c, n_c = mixed[2:]
            bo_c = _pool(p_ctx, cnt_ctx, w_bd, ps)
            ctx = _output(ctx, mods, ctx_row, a_c, bo_c, n_c, w_out_b[l], 256)
        x = x_new
    return x
```

```python
import functools
import math

import numpy as np
import jax
import jax.numpy as jnp
from jax import lax
from jax.experimental import pallas as pl
from jax.experimental.pallas import tpu as pltpu

F32 = jnp.float32
BF16 = jnp.bfloat16
F8 = jnp.float8_e4m3fn
P_SHIFT = 8.0
F8_MAX = 448.0
FP8_GAIN_LIMIT = 2.0
FP8_PEAK_GAIN = 0.98 * F8_MAX / (8.0 * 2.0 ** 0.5)

D_MODEL = 1024
DEPTH = 4
GRID_W = 64
HEAD_DIM = 64
ROPE_THETA = 10000.0
EPS = 1e-6
ATTN_SCALE = HEAD_DIM ** -0.5
LOG2E = 1.4426950408889634
Q_SCALE = ATTN_SCALE * LOG2E
POOL_WINDOWS = (2, 4, 8, 16)
NA_KH = 8
NA_KW = 16
IN_WIDTH = 2816

LANES = 128
GROUP_W = 256
HEADS_PER_GROUP = GROUP_W // HEAD_DIM
KV_CHUNK = 256
TILES_PER_ITER = 16

W_AQ, W_AK, W_AV, W_AG, W_BZ, W_BG, W_NQ, W_NK, W_NV, W_NG = (
    0, 512, 640, 768, 1280, 1536, 1792, 2048, 2304, 2560)
P_QA, P_KE, P_VE, P_GA, P_BZ, P_GB, P_NQ, P_NK, P_NV, P_GN = (
    0, 512, 1024, 1280, 1792, 2048, 2304, 2560, 2816, 3072)
P_WIDTH = 3328
MOD_ROWS = 24
TOKEN_TILE = 512
OUT_TILE = 1024
NEG = -1e30
VMEM_LIMIT = 48 * 1024 * 1024


def _silu(v):
    return v / (1.0 + jnp.exp(-v))


def _split_bf16(v):
    hi = v.astype(BF16)
    lo = (v - hi.astype(F32)).astype(BF16)
    return hi, lo


def _dot(a, b):
    return jnp.dot(a, b, preferred_element_type=F32)


def _mod_kernel(c_ref, w_ref, b_ref, o_ref):
    a_hi, a_lo = _split_bf16(_silu(c_ref[...]))
    w_hi, w_lo = _split_bf16(w_ref[0])
    o_ref[0] = _dot(a_hi, w_hi) + _dot(a_lo, w_hi) + _dot(a_hi, w_lo) + b_ref[0]


def _modulation(c, c_ctx, w_mod, b_mod):
    bsz = c.shape[0]
    cs = jnp.concatenate(
        [c, c_ctx[None], jnp.zeros((MOD_ROWS - bsz - 1, D_MODEL), F32)], axis=0)
    return pl.pallas_call(
        _mod_kernel,
        grid=(DEPTH, 3),
        in_specs=[
            pl.BlockSpec((MOD_ROWS, D_MODEL), lambda l, j: (0, 0)),
            pl.BlockSpec((1, D_MODEL, D_MODEL), lambda l, j: (l, 0, j)),
            pl.BlockSpec((1, 1, D_MODEL), lambda l, j: (l, 0, j)),
        ],
        out_specs=pl.BlockSpec((1, MOD_ROWS, D_MODEL), lambda l, j: (l, 0, j)),
        out_shape=jax.ShapeDtypeStruct((DEPTH, MOD_ROWS, 3 * D_MODEL), F32),
        compiler_params=pltpu.CompilerParams(vmem_limit_bytes=VMEM_LIMIT),
        name="modulation",
    )(cs, w_mod, b_mod.reshape(DEPTH, 1, 3 * D_MODEL))


def _proj_kernel(x_ref, shift_ref, scale_ref, ng_ref, w_ref, cos_ref, sa_ref, sb_ref,
                 aqg_ref, akg_ref, nqg_ref, nkg_ref, seg_ref, repk_ref, repv_ref, ones_ref,
                 o_ref, hb0_ref, hb1_ref, y0_ref, y1_ref, *, kv_only):
    step = pl.program_id(0)

    @pl.when(step == 0)
    def _():
        hb1_ref[...] = jnp.zeros_like(hb1_ref)
        y0_ref[...] = jnp.zeros_like(y0_ref)
        y1_ref[...] = jnp.zeros_like(y1_ref)

    def pre_norm(hb_ref):
        x = x_ref[0]
        ms = jnp.mean(x * x, axis=-1, keepdims=True)
        h = x * lax.rsqrt(ms + EPS) * ng_ref[...]
        h = h * (1.0 + scale_ref[0]) + shift_ref[0]
        hb_ref[...] = h.astype(BF16)

    def stage(hb_w, hb_r, y_w, y_r):
        pre_norm(hb_w)
        epilogue = _proj_epilogue(y_r, cos_ref, sa_ref, sb_ref, aqg_ref, akg_ref, nqg_ref,
                                  nkg_ref, seg_ref, repk_ref, repv_ref, ones_ref, o_ref, kv_only)
        for c0 in range(0, IN_WIDTH, GROUP_W):
            if not kv_only or c0 in (W_AK, W_NK, W_NV):
                y_w[:, c0:c0 + GROUP_W] = _dot(hb_r[...], w_ref[:, c0:c0 + GROUP_W])
            next(epilogue)

    @pl.when(step % 2 == 0)
    def _():
        stage(hb0_ref, hb1_ref, y1_ref, y0_ref)

    @pl.when(step % 2 == 1)
    def _():
        stage(hb1_ref, hb0_ref, y0_ref, y1_ref)


def _proj_epilogue(y_ref, cos_ref, sa_ref, sb_ref, aqg_ref, akg_ref, nqg_ref, nkg_ref,
                   seg_ref, repk_ref, repv_ref, ones_ref, o_ref, kv_only):
    def proj(c0, width):
        return y_ref[:, c0:c0 + width]

    def head_norm(y, g):
        ss = _dot((y * y).astype(BF16), seg_ref[...])
        return y * lax.rsqrt(ss * (1.0 / HEAD_DIM) + EPS) * g

    def rope(y):
        width = y.shape[-1]
        wide = lambda t: jnp.tile(t[...], (1, width // LANES))
        return (y * wide(cos_ref) + pltpu.roll(y, width - 16, 1) * wide(sa_ref)
                + pltpu.roll(y, 16, 1) * wide(sb_ref))

    def put(c0, v):
        o_ref[0, :, c0:c0 + v.shape[-1]] = v.astype(BF16)

    def unless_kv_only(c0, make):
        if kv_only:
            o_ref[0, :, c0:c0 + GROUP_W] = jnp.zeros((y_ref.shape[0], GROUP_W), BF16)
        else:
            put(c0, make())

    for j in range(2):
        unless_kv_only(P_QA + j * GROUP_W, lambda: rope(head_norm(
            proj(W_AQ + j * GROUP_W, GROUP_W), aqg_ref[...])) * Q_SCALE)
        yield
    kv = proj(W_AK, GROUP_W)
    kb = rope(head_norm(kv, akg_ref[...]))[:, :LANES].astype(BF16)
    put(P_KE, _dot(kb, repk_ref[...]))
    put(P_VE, _dot(kv[:, LANES:].astype(BF16), repv_ref[...]) + ones_ref[...])
    yield
    for j in range(2):
        unless_kv_only(P_GA + j * GROUP_W,
                       lambda: _silu(proj(W_AG + j * GROUP_W, GROUP_W)))
        yield
    unless_kv_only(P_BZ, lambda: proj(W_BZ, GROUP_W))
    yield
    unless_kv_only(P_GB, lambda: _silu(proj(W_BG, GROUP_W)))
    yield
    unless_kv_only(P_NQ, lambda: head_norm(proj(W_NQ, GROUP_W), nqg_ref[...]) * Q_SCALE)
    yield
    put(P_NK, head_norm(proj(W_NK, GROUP_W), nkg_ref[...]))
    yield
    put(P_NV, proj(W_NV, GROUP_W))
    yield
    unless_kv_only(P_GN, lambda: _silu(proj(W_NG, GROUP_W)))
    yield


def _projection(x, mods, mod_row, norm_gain, w_in, tables, gains, consts, tm, kv_only=False):
    bsz, length, _ = x.shape
    cos, sa, sb = tables
    per_seq = length // tm
    n_tiles = bsz * per_seq
    norm_tile = lambda i: jnp.minimum(i, n_tiles - 1)
    out_tile = lambda i: jnp.maximum(i - 2, 0)
    vec = lambda k: pl.BlockSpec(
        (1, 1, D_MODEL), lambda i: (mod_row(norm_tile(i) // per_seq), 0, k))
    const2 = lambda a: pl.BlockSpec(a.shape, lambda i: (0, 0))
    tab = pl.BlockSpec((tm, LANES), lambda i: (out_tile(i) % per_seq, 0))
    hidden = pltpu.VMEM((tm, D_MODEL), BF16)
    projected = pltpu.VMEM((tm, IN_WIDTH), F32)
    return pl.pallas_call(
        functools.partial(_proj_kernel, kv_only=kv_only),
        grid=(n_tiles + 2,),
        in_specs=[
            pl.BlockSpec((1, tm, D_MODEL),
                         lambda i: (norm_tile(i) // per_seq, norm_tile(i) % per_seq, 0)),
            vec(0), vec(1), const2(norm_gain), const2(w_in), tab, tab, tab,
            *[const2(g) for g in gains], *[const2(a) for a in consts],
        ],
        out_specs=pl.BlockSpec((1, tm, P_WIDTH),
                               lambda i: (out_tile(i) // per_seq, out_tile(i) % per_seq, 0)),
        out_shape=jax.ShapeDtypeStruct((bsz, length, P_WIDTH), BF16),
        scratch_shapes=[hidden, hidden, projected, projected],
        compiler_params=pltpu.CompilerParams(
            vmem_limit_bytes=VMEM_LIMIT, dimension_semantics=("arbitrary",)),
        name="projection",
    )(x, mods, mods, norm_gain, w_in, cos, sa, sb, *gains, *consts)


def _attn_kernel(*refs, tq, tiles_per_row, n_tiles, tiles_per_iter, n_kv, chunks, n_bias,
                 sums_in_v, fp8):
    q_ref, g_ref = refs[0], refs[1]
    kv = refs[2:2 + 2 * n_kv]
    pos = 2 + 2 * n_kv
    bias_ref = refs[pos] if n_bias else None
    pos += 1 if n_bias else 0
    o_ref, s_a, s_b, m_a, m_b, mc_ref, lc_ref = refs[pos:pos + 7]
    mm_dtype = BF16
    if fp8:
        mm_dtype = F8
        kv8 = refs[pos + 7:pos + 7 + 2 * n_kv]
        for src, dst in zip(kv, kv8):
            dst[...] = src[...].astype(F8)
        kv = kv8
    rows = HEADS_PER_GROUP * tq
    n_chunks = len(chunks)
    lane_head = jnp.right_shift(lax.broadcasted_iota(jnp.int32, (tq, GROUP_W), 1), 6)
    low_half = lax.broadcasted_iota(jnp.int32, (tq, LANES), 1) < HEAD_DIM
    shift = tiles_per_row.bit_length() - 1

    def tile_pos(t):
        if tiles_per_row == 1:
            return t, 0
        r0 = (t & (tiles_per_row - 1)) * tq
        return t >> shift, (r0 if isinstance(r0, int) else pl.multiple_of(r0, tq))

    def stage(t1, s1, m1, t0, s0, m0):
        if t1 is not None:
            b1, r1 = tile_pos(t1)
            q = q_ref[b1, pl.ds(r1, tq), :].astype(F32)
            qs = jnp.concatenate([jnp.where(lane_head == h, q, 0.0).astype(mm_dtype)
                                  for h in range(HEADS_PER_GROUP)], axis=0)
        if t0 is not None:
            b0, r0 = tile_pos(t0)
            m_prev = m0[...]
            m_prev = jnp.concatenate([m_prev, m_prev], axis=-1)
            acc = None
        for ci, (ai, off) in enumerate(chunks):
            if t1 is not None:
                k = kv[2 * ai][b1, off:off + KV_CHUNK, :]
                s = lax.dot_general(qs, k, (((1,), (1,)), ((), ())),
                                    preferred_element_type=F32)
                if ci < n_bias:
                    s = s + bias_ref[0, :, ci * KV_CHUNK:(ci + 1) * KV_CHUNK]
                s1[ci] = s
                mc_ref[ci] = jnp.maximum(s[:, :LANES], s[:, LANES:])
            if t0 is not None:
                z = s0[ci] - m_prev
                p = jnp.exp2(z.astype(BF16) if fp8 else z)
                if not sums_in_v:
                    lc_ref[ci] = (p[:, :LANES] + p[:, LANES:]).astype(F32)
                d = _dot(p.astype(mm_dtype), kv[2 * ai + 1][b0, off:off + KV_CHUNK, :])
                acc = d if acc is None else acc + d
        if t1 is not None:
            m = mc_ref[0]
            for ci in range(1, n_chunks):
                m = jnp.maximum(m, mc_ref[ci])
            m = jnp.max(m, axis=-1, keepdims=True) - (P_SHIFT if fp8 else 0.0)
            m1[...] = jnp.broadcast_to(m, (rows, LANES))
        if t0 is not None:
            gate = g_ref[b0, pl.ds(r0, tq), :].astype(F32)
            head = lambda a, h: a[h * tq:(h + 1) * tq, :]
            if sums_in_v:
                o = acc / pltpu.roll(acc, HEAD_DIM, 1)
                halves = [jnp.where(low_half, head(o, h), pltpu.roll(head(o, h + 1), HEAD_DIM, 1))
                          for h in (0, 2)]
                out = jnp.concatenate(halves, axis=-1)
            else:
                l = lc_ref[0]
                for ci in range(1, n_chunks):
                    l = l + lc_ref[ci]
                o = acc / jnp.sum(l, axis=-1, keepdims=True)
                out = jnp.zeros((tq, GROUP_W), F32)
                for h in range(HEADS_PER_GROUP):
                    out = out + jnp.where(lane_head == h, head(o, h), 0.0)
            o_ref[b0, pl.ds(r0, tq), :] = (out * gate).astype(BF16)

    stage(0, s_a, m_a, None, None, None)

    if tiles_per_iter == n_tiles:
        for t in range(0, n_tiles, 2):
            stage(t + 1, s_b, m_b, t, s_a, m_a)
            stage(t + 2 if t + 2 < n_tiles else None, s_a, m_a, t + 1, s_b, m_b)
        return

    def body(j, carry):
        t = tiles_per_iter * j
        for u in range(0, tiles_per_iter, 2):
            stage(t + u + 1, s_b, m_b, t + u, s_a, m_a)
            stage(jnp.minimum(t + u + 2, n_tiles - 1), s_a, m_a, t + u + 1, s_b, m_b)
        return carry

    lax.fori_loop(0, n_tiles // tiles_per_iter, body, 0)


def _attention(grid, nb, q_rows, q, gate, kvs, bias, out_shape, out_map, tq, chunks, name,
               fp8=False):
    n_kv = len(kvs) // 2
    n_bias = 0
    operands = [q[0], gate[0]]
    specs = [pl.BlockSpec((nb, q_rows, GROUP_W), q[1]),
             pl.BlockSpec((nb, q_rows, GROUP_W), gate[1])]
    for arr, kv_rows, kv_lanes, imap in kvs:
        operands.append(arr)
        specs.append(pl.BlockSpec((nb, kv_rows, kv_lanes), imap))
    sums_in_v = kvs[1][2] == LANES
    if bias is not None:
        arr, imap = bias
        n_bias = arr.shape[-1] // KV_CHUNK
        operands.append(arr)
        specs.append(pl.BlockSpec((1,) + arr.shape[1:], imap))
    rows = HEADS_PER_GROUP * tq
    tiles_per_row = q_rows // tq
    n_tiles = nb * tiles_per_row
    assert n_tiles % 2 == 0 and tiles_per_row & (tiles_per_row - 1) == 0
    score_buf = pltpu.VMEM((len(chunks), rows, KV_CHUNK), F32)
    max_buf = pltpu.VMEM((rows, LANES), F32)
    part_buf = pltpu.VMEM((len(chunks), rows, LANES), F32)
    sum_buf = pltpu.VMEM((1, 8, LANES) if sums_in_v else (len(chunks), rows, LANES), F32)
    kv8 = [pltpu.VMEM((nb, kv_rows, kv_lanes), F8) for _, kv_rows, kv_lanes, _ in kvs] if fp8 else []
    return pl.pallas_call(
        functools.partial(_attn_kernel, tq=tq, tiles_per_row=tiles_per_row, n_tiles=n_tiles,
                          tiles_per_iter=math.gcd(n_tiles, TILES_PER_ITER), n_kv=n_kv,
                          chunks=chunks, n_bias=n_bias, sums_in_v=sums_in_v, fp8=fp8),
        grid=grid,
        in_specs=specs,
        out_specs=pl.BlockSpec((nb, q_rows, GROUP_W), out_map),
        out_shape=jax.ShapeDtypeStruct(out_shape, BF16),
        scratch_shapes=[score_buf, score_buf, max_buf, max_buf, part_buf, sum_buf, *kv8],
        compiler_params=pltpu.CompilerParams(vmem_limit_bytes=VMEM_LIMIT),
        name=name,
    )(*operands)


def _col(off):
    return off // GROUP_W


GQA_TQ = 128
CTX_NB = 8


def _gqa_kv(p):
    rows = p.shape[1]
    return [(p, rows, GROUP_W, lambda b, j: (b, 0, _col(P_KE) + j)),
            (p, rows, LANES, lambda b, j: (b, 0, P_VE // LANES + j))]


def _gqa_latent(p, p_ctx, fp8):
    bsz, length, _ = p.shape
    n_ctx = p_ctx.shape[1]
    kvs = _gqa_kv(p) + _gqa_kv(p_ctx)
    chunks = tuple((0, c * KV_CHUNK) for c in range(length // KV_CHUNK)) + tuple(
        (1, c * KV_CHUNK) for c in range(n_ctx // KV_CHUNK))
    return _attention(
        grid=(bsz, 2), nb=1, q_rows=length,
        q=(p, lambda b, j: (b, 0, _col(P_QA) + j)),
        gate=(p, lambda b, j: (b, 0, _col(P_GA) + j)),
        kvs=kvs, bias=None,
        out_shape=(bsz, length, 2 * GROUP_W), out_map=lambda b, j: (b, 0, j),
        tq=GQA_TQ, chunks=chunks, name="gqa_attention", fp8=fp8)


def _gqa_context(p_ctx, fp8):
    bsz, n_ctx, _ = p_ctx.shape
    return _attention(
        grid=(bsz // CTX_NB, 2), nb=CTX_NB, q_rows=n_ctx,
        q=(p_ctx, lambda b, j: (b, 0, _col(P_QA) + j)),
        gate=(p_ctx, lambda b, j: (b, 0, _col(P_GA) + j)),
        kvs=_gqa_kv(p_ctx), bias=None,
        out_shape=(bsz, n_ctx, 2 * GROUP_W), out_map=lambda b, j: (b, 0, j),
        tq=GQA_TQ, chunks=tuple((0, c * KV_CHUNK) for c in range(n_ctx // KV_CHUNK)),
        name="gqa_context_attention", fp8=fp8)


NA_TQ = 256
NA_BAND = 3
NA_GROUPS = 2048 // NA_TQ
NA_NB = 4
NA_PATTERN_GROUPS = (0, 1, NA_GROUPS - 1)


def _na_band_start(g):
    return jnp.clip(g - 1, 0, NA_GROUPS - NA_BAND)


def _na_latent(p, p_ctx, bias, fp8):
    bsz, length, _ = p.shape
    kvs = []
    for i in range(NA_BAND):
        kvs.append((p, KV_CHUNK, GROUP_W,
                    lambda g, b, i=i: (b, _na_band_start(g) + i, _col(P_NK))))
        kvs.append((p, KV_CHUNK, GROUP_W,
                    lambda g, b, i=i: (b, _na_band_start(g) + i, _col(P_NV))))
    n_ctx = p_ctx.shape[1]
    kvs += [(p_ctx, n_ctx, GROUP_W, lambda g, b: (b, 0, _col(P_NK))),
            (p_ctx, n_ctx, GROUP_W, lambda g, b: (b, 0, _col(P_NV)))]
    chunks = tuple((i, 0) for i in range(NA_BAND)) + tuple(
        (NA_BAND, c * KV_CHUNK) for c in range(n_ctx // KV_CHUNK))
    pattern = lambda g, b: (jnp.where(g == 0, 0, jnp.where(g == NA_GROUPS - 1, 2, 1)), 0, 0)
    return _attention(
        grid=(NA_GROUPS, bsz // NA_NB), nb=NA_NB, q_rows=NA_TQ,
        q=(p, lambda g, b: (b, g, _col(P_NQ))),
        gate=(p, lambda g, b: (b, g, _col(P_GN))),
        kvs=kvs, bias=(bias, pattern),
        out_shape=(bsz, length, GROUP_W), out_map=lambda g, b: (b, g, 0),
        tq=NA_TQ, chunks=chunks, name="na_attention", fp8=fp8)


def _na_context(p_ctx, fp8):
    bsz, n_ctx, _ = p_ctx.shape
    kvs = [(p_ctx, n_ctx, GROUP_W, lambda b: (b, 0, _col(P_NK))),
           (p_ctx, n_ctx, GROUP_W, lambda b: (b, 0, _col(P_NV)))]
    return _attention(
        grid=(bsz // CTX_NB,), nb=CTX_NB, q_rows=n_ctx,
        q=(p_ctx, lambda b: (b, 0, _col(P_NQ))),
        gate=(p_ctx, lambda b: (b, 0, _col(P_GN))),
        kvs=kvs, bias=None,
        out_shape=(bsz, n_ctx, GROUP_W), out_map=lambda b: (b, 0, 0),
        tq=NA_TQ, chunks=tuple((0, c * KV_CHUNK) for c in range(n_ctx // KV_CHUNK)),
        name="na_context_attention", fp8=fp8)


def _na_bias_layout():
    rows = 2048 // GRID_W
    rows_per_tile, rows_per_chunk = NA_TQ // GRID_W, KV_CHUNK // GRID_W
    band_rows = NA_BAND * rows_per_chunk
    slot = np.full((len(NA_PATTERN_GROUPS), rows_per_tile, band_rows), 2 * NA_KH - 1)
    for pat, g in enumerate(NA_PATTERN_GROUPS):
        band0 = rows_per_chunk * min(max(g - 1, 0), NA_GROUPS - NA_BAND)
        for qr in range(rows_per_tile):
            r = rows_per_tile * g + qr
            r0 = min(max(r - NA_KH // 2, 0), rows - NA_KH)
            for kr in range(band_rows):
                rp = band0 + kr
                if r0 <= rp < r0 + NA_KH:
                    slot[pat, qr, kr] = rp - r + NA_KH - 1
    c = np.arange(GRID_W)
    c0 = np.clip(c - NA_KW // 2, 0, GRID_W - NA_KW)
    col_ok = (c[None, :] >= c0[:, None]) & (c[None, :] < c0[:, None] + NA_KW)
    return slot, col_ok


def _na_bias(rpb, slot, col_ok):
    heads, n_dr, _ = rpb.shape
    rpb = rpb * LOG2E
    w = jnp.concatenate([rpb[..., NA_KW - 1:],
                         jnp.full((heads, n_dr, LANES - (2 * NA_KW - 1)), NEG, F32),
                         rpb[..., :NA_KW - 1]], axis=-1)
    t = jnp.tile(w, (1, 1, GRID_W))[..., :GRID_W * (LANES - 1)]
    t = t.reshape(heads, n_dr, GRID_W, LANES - 1)[..., :GRID_W]
    t = jnp.where(col_ok[None, None], t, NEG)
    t = jnp.concatenate([t, jnp.full((heads, 1, GRID_W, GRID_W), NEG, F32)], axis=1)
    n_pat, n_qr, n_kr = slot.shape
    blocks = jnp.stack([t[:, int(s)] for s in slot.reshape(-1)], axis=1)
    blocks = blocks.reshape(heads, n_pat, n_qr, n_kr, GRID_W, GRID_W)
    blocks = jnp.transpose(blocks, (1, 0, 2, 4, 3, 5))
    return blocks.reshape(n_pat, heads * n_qr * GRID_W, n_kr * GRID_W)


POOL_HALO = 8
POOL_ROWS = 256


def _pool_kernel(z_ref, g_ref, cnt_ref, w_ref, s_ref, o_ref, zp_ref, *, length):
    zeros = jnp.zeros((POOL_HALO, GROUP_W), F32)
    zp_ref[0:POOL_HALO, :] = zeros
    zp_ref[POOL_HALO + length:2 * POOL_HALO + length, :] = zeros
    zp_ref[POOL_HALO:POOL_HALO + length, :] = z_ref[0].astype(F32)
    first = lax.broadcasted_iota(jnp.int32, (POOL_ROWS, LANES), 1) < HEAD_DIM

    def window(t0, lanes, offsets):
        acc = None
        for o in offsets:
            v = zp_ref[POOL_HALO + t0 + o:POOL_HALO + t0 + o + POOL_ROWS, lanes]
            acc = v if acc is None else acc + v
        return acc

    for t0 in range(0, length, POOL_ROWS):
        lo, hi = slice(0, LANES), slice(LANES, 2 * LANES)
        s2 = window(t0, lo, (-1, 0))
        s4 = s2 + window(t0, lo, (-2, 1))
        s8 = window(t0, hi, range(-4, 4))
        s16 = s8 + window(t0, hi, tuple(range(-8, -4)) + tuple(range(4, 8)))
        win = jnp.concatenate([jnp.where(first, s2, s4), jnp.where(first, s8, s16)], axis=-1)
        rows = slice(t0, t0 + POOL_ROWS)
        z = zp_ref[POOL_HALO + t0:POOL_HALO + t0 + POOL_ROWS, :]
        pooled = win / cnt_ref[rows, :] - z
        y = _dot(pooled.astype(BF16), w_ref[...]) * s_ref[...]
        o_ref[0, rows, :] = (y * g_ref[0, rows, :].astype(F32)).astype(BF16)


def _pool(p, cnt, w_bd, scale):
    bsz, length, _ = p.shape
    return pl.pallas_call(
        functools.partial(_pool_kernel, length=length),
        grid=(bsz,),
        in_specs=[
            pl.BlockSpec((1, length, GROUP_W), lambda b: (b, 0, _col(P_BZ))),
            pl.BlockSpec((1, length, GROUP_W), lambda b: (b, 0, _col(P_GB))),
            pl.BlockSpec((length, GROUP_W), lambda b: (0, 0)),
            pl.BlockSpec((GROUP_W, GROUP_W), lambda b: (0, 0)),
            pl.BlockSpec((1, GROUP_W), lambda b: (0, 0)),
        ],
        out_specs=pl.BlockSpec((1, length, GROUP_W), lambda b: (b, 0, 0)),
        out_shape=jax.ShapeDtypeStruct((bsz, length, GROUP_W), BF16),
        scratch_shapes=[pltpu.VMEM((length + 2 * POOL_HALO, GROUP_W), F32)],
        compiler_params=pltpu.CompilerParams(vmem_limit_bytes=VMEM_LIMIT),
        name="pool_mixer",
    )(p, p, cnt, w_bd, scale)


def _pool_counts(length):
    t = np.arange(length)
    cols = []
    for w in POOL_WINDOWS:
        lo = np.maximum(t - w // 2, 0)
        hi = np.minimum(t + w // 2 - 1, length - 1)
        cols.append(np.repeat((hi - lo + 1).astype(np.float32)[:, None], HEAD_DIM, axis=1))
    return np.concatenate(cols, axis=1)


def _out_kernel(x_ref, gate_ref, a_ref, b_ref, n_ref, w_ref, o_ref):
    y = (_dot(a_ref[0], w_ref[0:512, :]) + _dot(b_ref[0], w_ref[512:768, :])
         + _dot(n_ref[0], w_ref[768:1024, :]))
    o_ref[0] = x_ref[0] + gate_ref[0] * y


def _output(x, mods, mod_row, a, b, n, w_out, tm):
    bsz, length, _ = x.shape
    tok = lambda width: pl.BlockSpec((1, tm, width), lambda bb, i: (bb, i, 0))
    return pl.pallas_call(
        _out_kernel,
        grid=(bsz, length // tm),
        in_specs=[
            tok(D_MODEL),
            pl.BlockSpec((1, 1, D_MODEL), lambda bb, i: (mod_row(bb), 0, 2)),
            tok(2 * GROUP_W), tok(GROUP_W), tok(GROUP_W),
            pl.BlockSpec((D_MODEL, D_MODEL), lambda bb, i: (0, 0)),
        ],
        out_specs=tok(D_MODEL),
        out_shape=jax.ShapeDtypeStruct(x.shape, F32),
        compiler_params=pltpu.CompilerParams(vmem_limit_bytes=VMEM_LIMIT),
        name="output_projection",
    )(x, mods, a, b, n, w_out)


def _rope_tables(seq):
    t = jnp.arange(seq)
    half = HEAD_DIM // 4
    inv_freq = ROPE_THETA ** (-jnp.arange(half, dtype=jnp.float32) / half)
    cos, sa, sb = [], [], []
    zero = jnp.zeros((seq, half), F32)
    for pos in (t // GRID_W, t % GRID_W):
        ang = pos.astype(jnp.float32)[:, None] * inv_freq[None, :]
        cos += [jnp.cos(ang), jnp.cos(ang)]
        sa += [-jnp.sin(ang), zero]
        sb += [zero, jnp.sin(ang)]
    tile = lambda parts: jnp.tile(jnp.concatenate(parts, axis=-1), (1, LANES // HEAD_DIM))
    return tile(cos), tile(sa), tile(sb)


def _layout_constants():
    lane = np.arange(GROUP_W)
    seg = lane[:, None] // HEAD_DIM == lane[None, :] // HEAD_DIM
    src = np.arange(LANES)[:, None]
    col = np.arange(2 * GROUP_W)[None, :]
    rep_k = src == (col // GROUP_W) * HEAD_DIM + col % HEAD_DIM
    colv = np.arange(2 * LANES)[None, :]
    is_v = colv % LANES < HEAD_DIM
    rep_v = is_v & (src == (colv // LANES) * HEAD_DIM + colv % LANES)
    return (jnp.asarray(seg, BF16), jnp.asarray(rep_k, BF16), jnp.asarray(rep_v, BF16),
            jnp.asarray(~is_v, F32))


def kernel(x, c, ctx, c_ctx, norm_gain, w_mod, b_mod, w_in, att_q_gain, att_k_gain,
           pool_w, pool_scale, na_q_gain, na_k_gain, na_rpb, w_out):
    bsz, seq, _ = x.shape
    n_ctx = ctx.shape[1]
    assert seq == NA_GROUPS * NA_TQ and n_ctx % KV_CHUNK == 0 and bsz < MOD_ROWS
    assert bsz % CTX_NB == 0 and bsz % NA_NB == 0

    mods_all = _modulation(c, c_ctx, w_mod, b_mod)
    tables = _rope_tables(seq)
    tables_ctx = (jnp.ones((n_ctx, LANES), F32), jnp.zeros((n_ctx, LANES), F32),
                  jnp.zeros((n_ctx, LANES), F32))
    consts = _layout_constants()
    cnt, cnt_ctx = jnp.asarray(_pool_counts(seq)), jnp.asarray(_pool_counts(n_ctx))
    slot, col_ok = _na_bias_layout()
    w_in_b = w_in.astype(BF16)
    w_out_b = w_out.astype(BF16)
    wide = lambda g: jnp.tile(g, GROUP_W // HEAD_DIM)[None, :]
    lat_row = lambda b: b
    ctx_row = lambda b: bsz

    for l in range(DEPTH):
        last = l == DEPTH - 1
        mods = mods_all[l].reshape(MOD_ROWS, 1, 3 * D_MODEL)
        gains = (wide(att_q_gain[l]), wide(att_k_gain[l]), wide(na_q_gain[l]),
                 wide(na_k_gain[l]))
        ng = norm_gain[l][None, :]
        p = _projection(x, mods, lat_row, ng, w_in_b[l], tables, gains, consts, TOKEN_TILE)
        p_ctx = _projection(ctx, mods, ctx_row, ng, w_in_b[l], tables_ctx, gains, consts, 256,
                            kv_only=last)
        bias = _na_bias(na_rpb[l], slot, col_ok)
        w_bd = jax.scipy.linalg.block_diag(*[pool_w[l, g] for g in range(4)]).astype(BF16)
        ps = pool_scale[l][None, :]

        peak = lambda g: jnp.max(jnp.abs(g[l]))
        gains_ok = jnp.maximum(peak(att_q_gain) * peak(att_k_gain),
                               peak(na_q_gain) * peak(na_k_gain)) <= FP8_GAIN_LIMIT
        for g in (att_q_gain, att_k_gain, na_q_gain, na_k_gain):
            gains_ok &= peak(g) <= FP8_PEAK_GAIN
        value_peak = functools.reduce(jnp.maximum, [
            jnp.max(jnp.abs(arr[:, :, c0:c0 + GROUP_W]))
            for arr in (p, p_ctx) for c0 in (P_VE, P_NV)])
        use_fp8 = gains_ok & (value_peak <= F8_MAX)

        def mixers(fp8):
            outs = (_gqa_latent(p, p_ctx, fp8), _na_latent(p, p_ctx, bias, fp8))
            if not last:
                outs += (_gqa_context(p_ctx, fp8), _na_context(p_ctx, fp8))
            return outs

        mixed = lax.cond(use_fp8, lambda: mixers(True), lambda: mixers(False))
        a, n = mixed[:2]
        bo = _pool(p, cnt, w_bd, ps)
        x_new = _output(x, mods, lat_row, a, bo, n, w_out_b[l], OUT_TILE)

        if not last:
            a_c, n_c = mixed[2:]
            bo_c = _pool(p_ctx, cnt_ctx, w_bd, ps)
            ctx = _output(ctx, mods, ctx_row, a_c, bo_c, n_c, w_out_b[l], 256)
        x = x_new
    return x
```

```python
import functools
import math

import numpy as np
import jax
import jax.numpy as jnp
from jax import lax
from jax.experimental import pallas as pl
from jax.experimental.pallas import tpu as pltpu

F32 = jnp.float32
BF16 = jnp.bfloat16
F8 = jnp.float8_e4m3fn
P_SHIFT = 8.0
F8_MAX = 448.0
FP8_GAIN_LIMIT = 2.0
FP8_PEAK_GAIN = 0.98 * F8_MAX / (8.0 * 2.0 ** 0.5)

D_MODEL = 1024
DEPTH = 4
GRID_W = 64
HEAD_DIM = 64
ROPE_THETA = 10000.0
EPS = 1e-6
ATTN_SCALE = HEAD_DIM ** -0.5
LOG2E = 1.4426950408889634
Q_SCALE = ATTN_SCALE * LOG2E
POOL_WINDOWS = (2, 4, 8, 16)
NA_KH = 8
NA_KW = 16
IN_WIDTH = 2816

LANES = 128
GROUP_W = 256
HEADS_PER_GROUP = GROUP_W // HEAD_DIM
KV_CHUNK = 256
TILES_PER_ITER = 16

W_AQ, W_AK, W_AV, W_AG, W_BZ, W_BG, W_NQ, W_NK, W_NV, W_NG = (
    0, 512, 640, 768, 1280, 1536, 1792, 2048, 2304, 2560)
P_QA, P_KE, P_VE, P_GA, P_BZ, P_GB, P_NQ, P_NK, P_NV, P_GN = (
    0, 512, 1024, 1280, 1792, 2048, 2304, 2560, 2816, 3072)
P_WIDTH = 3328
MOD_ROWS = 24
TOKEN_TILE = 512
OUT_TILE = 1024
NEG = -1e30
VMEM_LIMIT = 48 * 1024 * 1024


def _silu(v):
    return v / (1.0 + jnp.exp(-v))


def _split_bf16(v):
    hi = v.astype(BF16)
    lo = (v - hi.astype(F32)).astype(BF16)
    return hi, lo


def _dot(a, b):
    return jnp.dot(a, b, preferred_element_type=F32)


def _mod_kernel(c_ref, w_ref, b_ref, o_ref):
    a_hi, a_lo = _split_bf16(_silu(c_ref[...]))
    w_hi, w_lo = _split_bf16(w_ref[0])
    o_ref[0] = _dot(a_hi, w_hi) + _dot(a_lo, w_hi) + _dot(a_hi, w_lo) + b_ref[0]


def _modulation(c, c_ctx, w_mod, b_mod):
    bsz = c.shape[0]
    cs = jnp.concatenate(
        [c, c_ctx[None], jnp.zeros((MOD_ROWS - bsz - 1, D_MODEL), F32)], axis=0)
    return pl.pallas_call(
        _mod_kernel,
        grid=(DEPTH, 3),
        in_specs=[
            pl.BlockSpec((MOD_ROWS, D_MODEL), lambda l, j: (0, 0)),
            pl.BlockSpec((1, D_MODEL, D_MODEL), lambda l, j: (l, 0, j)),
            pl.BlockSpec((1, 1, D_MODEL), lambda l, j: (l, 0, j)),
        ],
        out_specs=pl.BlockSpec((1, MOD_ROWS, D_MODEL), lambda l, j: (l, 0, j)),
        out_shape=jax.ShapeDtypeStruct((DEPTH, MOD_ROWS, 3 * D_MODEL), F32),
        compiler_params=pltpu.CompilerParams(vmem_limit_bytes=VMEM_LIMIT),
        name="modulation",
    )(cs, w_mod, b_mod.reshape(DEPTH, 1, 3 * D_MODEL))


def _proj_kernel(x_ref, shift_ref, scale_ref, ng_ref, w_ref, cos_ref, sa_ref, sb_ref,
                 aqg_ref, akg_ref, nqg_ref, nkg_ref, seg_ref, repk_ref, repv_ref, ones_ref,
                 o_ref, hb0_ref, hb1_ref, y0_ref, y1_ref, *, kv_only):
    step = pl.program_id(0)

    @pl.when(step == 0)
    def _():
        hb1_ref[...] = jnp.zeros_like(hb1_ref)
        y0_ref[...] = jnp.zeros_like(y0_ref)
        y1_ref[...] = jnp.zeros_like(y1_ref)

    def pre_norm(hb_ref):
        x = x_ref[0]
        ms = jnp.mean(x * x, axis=-1, keepdims=True)
        h = x * lax.rsqrt(ms + EPS) * ng_ref[...]
        h = h * (1.0 + scale_ref[0]) + shift_ref[0]
        hb_ref[...] = h.astype(BF16)

    def stage(hb_w, hb_r, y_w, y_r):
        pre_norm(hb_w)
        epilogue = _proj_epilogue(y_r, cos_ref, sa_ref, sb_ref, aqg_ref, akg_ref, nqg_ref,
                                  nkg_ref, seg_ref, repk_ref, repv_ref, ones_ref, o_ref, kv_only)
        for c0 in range(0, IN_WIDTH, GROUP_W):
            if not kv_only or c0 in (W_AK, W_NK, W_NV):
                y_w[:, c0:c0 + GROUP_W] = _dot(hb_r[...], w_ref[:, c0:c0 + GROUP_W])
            next(epilogue)

    @pl.when(step % 2 == 0)
    def _():
        stage(hb0_ref, hb1_ref, y1_ref, y0_ref)

    @pl.when(step % 2 == 1)
    def _():
        stage(hb1_ref, hb0_ref, y0_ref, y1_ref)


def _proj_epilogue(y_ref, cos_ref, sa_ref, sb_ref, aqg_ref, akg_ref, nqg_ref, nkg_ref,
                   seg_ref, repk_ref, repv_ref, ones_ref, o_ref, kv_only):
    def proj(c0, width):
        return y_ref[:, c0:c0 + width]

    def head_norm(y, g):
        ss = _dot((y * y).astype(BF16), seg_ref[...])
        return y * lax.rsqrt(ss * (1.0 / HEAD_DIM) + EPS) * g

    def rope(y):
        width = y.shape[-1]
        wide = lambda t: jnp.tile(t[...], (1, width // LANES))
        return (y * wide(cos_ref) + pltpu.roll(y, width - 16, 1) * wide(sa_ref)
                + pltpu.roll(y, 16, 1) * wide(sb_ref))

    def put(c0, v):
        o_ref[0, :, c0:c0 + v.shape[-1]] = v.astype(BF16)

    def unless_kv_only(c0, make):
        if kv_only:
            o_ref[0, :, c0:c0 + GROUP_W] = jnp.zeros((y_ref.shape[0], GROUP_W), BF16)
        else:
            put(c0, make())

    for j in range(2):
        unless_kv_only(P_QA + j * GROUP_W, lambda: rope(head_norm(
            proj(W_AQ + j * GROUP_W, GROUP_W), aqg_ref[...])) * Q_SCALE)
        yield
    kv = proj(W_AK, GROUP_W)
    kb = rope(head_norm(kv, akg_ref[...]))[:, :LANES].astype(BF16)
    put(P_KE, _dot(kb, repk_ref[...]))
    put(P_VE, _dot(kv[:, LANES:].astype(BF16), repv_ref[...]) + ones_ref[...])
    yield
    for j in range(2):
        unless_kv_only(P_GA + j * GROUP_W,
                       lambda: _silu(proj(W_AG + j * GROUP_W, GROUP_W)))
        yield
    unless_kv_only(P_BZ, lambda: proj(W_BZ, GROUP_W))
    yield
    unless_kv_only(P_GB, lambda: _silu(proj(W_BG, GROUP_W)))
    yield
    unless_kv_only(P_NQ, lambda: head_norm(proj(W_NQ, GROUP_W), nqg_ref[...]) * Q_SCALE)
    yield
    put(P_NK, head_norm(proj(W_NK, GROUP_W), nkg_ref[...]))
    yield
    put(P_NV, proj(W_NV, GROUP_W))
    yield
    unless_kv_only(P_GN, lambda: _silu(proj(W_NG, GROUP_W)))
    yield


def _projection(x, mods, mod_row, norm_gain, w_in, tables, gains, consts, tm, kv_only=False):
    bsz, length, _ = x.shape
    cos, sa, sb = tables
    per_seq = length // tm
    n_tiles = bsz * per_seq
    norm_tile = lambda i: jnp.minimum(i, n_tiles - 1)
    out_tile = lambda i: jnp.maximum(i - 2, 0)
    vec = lambda k: pl.BlockSpec(
        (1, 1, D_MODEL), lambda i: (mod_row(norm_tile(i) // per_seq), 0, k))
    const2 = lambda a: pl.BlockSpec(a.shape, lambda i: (0, 0))
    tab = pl.BlockSpec((tm, LANES), lambda i: (out_tile(i) % per_seq, 0))
    hidden = pltpu.VMEM((tm, D_MODEL), BF16)
    projected = pltpu.VMEM((tm, IN_WIDTH), F32)
    return pl.pallas_call(
        functools.partial(_proj_kernel, kv_only=kv_only),
        grid=(n_tiles + 2,),
        in_specs=[
            pl.BlockSpec((1, tm, D_MODEL),
                         lambda i: (norm_tile(i) // per_seq, norm_tile(i) % per_seq, 0)),
            vec(0), vec(1), const2(norm_gain), const2(w_in), tab, tab, tab,
            *[const2(g) for g in gains], *[const2(a) for a in consts],
        ],
        out_specs=pl.BlockSpec((1, tm, P_WIDTH),
                               lambda i: (out_tile(i) // per_seq, out_tile(i) % per_seq, 0)),
        out_shape=jax.ShapeDtypeStruct((bsz, length, P_WIDTH), BF16),
        scratch_shapes=[hidden, hidden, projected, projected],
        compiler_params=pltpu.CompilerParams(
            vmem_limit_bytes=VMEM_LIMIT, dimension_semantics=("arbitrary",)),
        name="projection",
    )(x, mods, mods, norm_gain, w_in, cos, sa, sb, *gains, *consts)


def _attn_kernel(*refs, tq, tiles_per_row, n_tiles, tiles_per_iter, n_kv, chunks, n_bias,
                 sums_in_v, fp8):
    q_ref, g_ref = refs[0], refs[1]
    kv = refs[2:2 + 2 * n_kv]
    pos = 2 + 2 * n_kv
    bias_ref = refs[pos] if n_bias else None
    pos += 1 if n_bias else 0
    o_ref, s_a, s_b, m_a, m_b, mc_ref, lc_ref = refs[pos:pos + 7]
    mm_dtype = BF16
    if fp8:
        mm_dtype = F8
        kv8 = refs[pos + 7:pos + 7 + 2 * n_kv]
        for src, dst in zip(kv, kv8):
            dst[...] = src[...].astype(F8)
        kv = kv8
    rows = HEADS_PER_GROUP * tq
    n_chunks = len(chunks)
    lane_head = jnp.right_shift(lax.broadcasted_iota(jnp.int32, (tq, GROUP_W), 1), 6)
    low_half = lax.broadcasted_iota(jnp.int32, (tq, LANES), 1) < HEAD_DIM
    shift = tiles_per_row.bit_length() - 1

    def tile_pos(t):
        if tiles_per_row == 1:
            return t, 0
        r0 = (t & (tiles_per_row - 1)) * tq
        return t >> shift, (r0 if isinstance(r0, int) else pl.multiple_of(r0, tq))

    def stage(t1, s1, m1, t0, s0, m0):
        if t1 is not None:
            b1, r1 = tile_pos(t1)
            q = q_ref[b1, pl.ds(r1, tq), :].astype(F32)
            qs = jnp.concatenate([jnp.where(lane_head == h, q, 0.0).astype(mm_dtype)
                                  for h in range(HEADS_PER_GROUP)], axis=0)
        if t0 is not None:
            b0, r0 = tile_pos(t0)
            m_prev = m0[...]
            m_prev = jnp.concatenate([m_prev, m_prev], axis=-1)
            acc = None
        for ci, (ai, off) in enumerate(chunks):
            if t1 is not None:
                k = kv[2 * ai][b1, off:off + KV_CHUNK, :]
                s = lax.dot_general(qs, k, (((1,), (1,)), ((), ())),
                                    preferred_element_type=F32)
                if ci < n_bias:
                    s = s + bias_ref[0, :, ci * KV_CHUNK:(ci + 1) * KV_CHUNK]
                s1[ci] = s
                mc_ref[ci] = jnp.maximum(s[:, :LANES], s[:, LANES:])
            if t0 is not None:
                z = s0[ci] - m_prev
                p = jnp.exp2(z.astype(BF16) if fp8 else z)
                if not sums_in_v:
                    lc_ref[ci] = (p[:, :LANES] + p[:, LANES:]).astype(F32)
                d = _dot(p.astype(mm_dtype), kv[2 * ai + 1][b0, off:off + KV_CHUNK, :])
                acc = d if acc is None else acc + d
        if t1 is not None:
            m = mc_ref[0]
            for ci in range(1, n_chunks):
                m = jnp.maximum(m, mc_ref[ci])
            m = jnp.max(m, axis=-1, keepdims=True) - (P_SHIFT if fp8 else 0.0)
            m1[...] = jnp.broadcast_to(m, (rows, LANES))
        if t0 is not None:
            gate = g_ref[b0, pl.ds(r0, tq), :].astype(F32)
            head = lambda a, h: a[h * tq:(h + 1) * tq, :]
            if sums_in_v:
                o = acc / pltpu.roll(acc, HEAD_DIM, 1)
                halves = [jnp.where(low_half, head(o, h), pltpu.roll(head(o, h + 1), HEAD_DIM, 1))
                          for h in (0, 2)]
                out = jnp.concatenate(halves, axis=-1)
            else:
                l = lc_ref[0]
                for ci in range(1, n_chunks):
                    l = l + lc_ref[ci]
                o = acc / jnp.sum(l, axis=-1, keepdims=True)
                out = jnp.zeros((tq, GROUP_W), F32)
                for h in range(HEADS_PER_GROUP):
                    out = out + jnp.where(lane_head == h, head(o, h), 0.0)
            o_ref[b0, pl.ds(r0, tq), :] = (out * gate).astype(BF16)

    stage(0, s_a, m_a, None, None, None)

    if tiles_per_iter == n_tiles:
        for t in range(0, n_tiles, 2):
            stage(t + 1, s_b, m_b, t, s_a, m_a)
            stage(t + 2 if t + 2 < n_tiles else None, s_a, m_a, t + 1, s_b, m_b)
        return

    def body(j, carry):
        t = tiles_per_iter * j
        for u in range(0, tiles_per_iter, 2):
            stage(t + u + 1, s_b, m_b, t + u, s_a, m_a)
            stage(jnp.minimum(t + u + 2, n_tiles - 1), s_a, m_a, t + u + 1, s_b, m_b)
        return carry

    lax.fori_loop(0, n_tiles // tiles_per_iter, body, 0)


def _attention(grid, nb, q_rows, q, gate, kvs, bias, out_shape, out_map, tq, chunks, name,
               fp8=False):
    n_kv = len(kvs) // 2
    n_bias = 0
    operands = [q[0], gate[0]]
    specs = [pl.BlockSpec((nb, q_rows, GROUP_W), q[1]),
             pl.BlockSpec((nb, q_rows, GROUP_W), gate[1])]
    for arr, kv_rows, kv_lanes, imap in kvs:
        operands.append(arr)
        specs.append(pl.BlockSpec((nb, kv_rows, kv_lanes), imap))
    sums_in_v = kvs[1][2] == LANES
    if bias is not None:
        arr, imap = bias
        n_bias = arr.shape[-1] // KV_CHUNK
        operands.append(arr)
        specs.append(pl.BlockSpec((1,) + arr.shape[1:], imap))
    rows = HEADS_PER_GROUP * tq
    tiles_per_row = q_rows // tq
    n_tiles = nb * tiles_per_row
    assert n_tiles % 2 == 0 and tiles_per_row & (tiles_per_row - 1) == 0
    score_buf = pltpu.VMEM((len(chunks), rows, KV_CHUNK), F32)
    max_buf = pltpu.VMEM((rows, LANES), F32)
    part_buf = pltpu.VMEM((len(chunks), rows, LANES), F32)
    sum_buf = pltpu.VMEM((1, 8, LANES) if sums_in_v else (len(chunks), rows, LANES), F32)
    kv8 = [pltpu.VMEM((nb, kv_rows, kv_lanes), F8) for _, kv_rows, kv_lanes, _ in kvs] if fp8 else []
    return pl.pallas_call(
        functools.partial(_attn_kernel, tq=tq, tiles_per_row=tiles_per_row, n_tiles=n_tiles,
                          tiles_per_iter=math.gcd(n_tiles, TILES_PER_ITER), n_kv=n_kv,
                          chunks=chunks, n_bias=n_bias, sums_in_v=sums_in_v, fp8=fp8),
        grid=grid,
        in_specs=specs,
        out_specs=pl.BlockSpec((nb, q_rows, GROUP_W), out_map),
        out_shape=jax.ShapeDtypeStruct(out_shape, BF16),
        scratch_shapes=[score_buf, score_buf, max_buf, max_buf, part_buf, sum_buf, *kv8],
        compiler_params=pltpu.CompilerParams(vmem_limit_bytes=VMEM_LIMIT),
        name=name,
    )(*operands)


def _col(off):
    return off // GROUP_W


GQA_TQ = 128
CTX_NB = 8


def _gqa_kv(p):
    rows = p.shape[1]
    return [(p, rows, GROUP_W, lambda b, j: (b, 0, _col(P_KE) + j)),
            (p, rows, LANES, lambda b, j: (b, 0, P_VE // LANES + j))]


def _gqa_latent(p, p_ctx, fp8):
    bsz, length, _ = p.shape
    n_ctx = p_ctx.shape[1]
    kvs = _gqa_kv(p) + _gqa_kv(p_ctx)
    chunks = tuple((0, c * KV_CHUNK) for c in range(length // KV_CHUNK)) + tuple(
        (1, c * KV_CHUNK) for c in range(n_ctx // KV_CHUNK))
    return _attention(
        grid=(bsz, 2), nb=1, q_rows=length,
        q=(p, lambda b, j: (b, 0, _col(P_QA) + j)),
        gate=(p, lambda b, j: (b, 0, _col(P_GA) + j)),
        kvs=kvs, bias=None,
        out_shape=(bsz, length, 2 * GROUP_W), out_map=lambda b, j: (b, 0, j),
        tq=GQA_TQ, chunks=chunks, name="gqa_attention", fp8=fp8)


def _gqa_context(p_ctx, fp8):
    bsz, n_ctx, _ = p_ctx.shape
    return _attention(
        grid=(bsz // CTX_NB, 2), nb=CTX_NB, q_rows=n_ctx,
        q=(p_ctx, lambda b, j: (b, 0, _col(P_QA) + j)),
        gate=(p_ctx, lambda b, j: (b, 0, _col(P_GA) + j)),
        kvs=_gqa_kv(p_ctx), bias=None,
        out_shape=(bsz, n_ctx, 2 * GROUP_W), out_map=lambda b, j: (b, 0, j),
        tq=GQA_TQ, chunks=tuple((0, c * KV_CHUNK) for c in range(n_ctx // KV_CHUNK)),
        name="gqa_context_attention", fp8=fp8)


NA_TQ = 256
NA_BAND = 3
NA_GROUPS = 2048 // NA_TQ
NA_NB = 4
NA_PATTERN_GROUPS = (0, 1, NA_GROUPS - 1)


def _na_band_start(g):
    return jnp.clip(g - 1, 0, NA_GROUPS - NA_BAND)


def _na_latent(p, p_ctx, bias, fp8):
    bsz, length, _ = p.shape
    kvs = []
    for i in range(NA_BAND):
        kvs.append((p, KV_CHUNK, GROUP_W,
                    lambda g, b, i=i: (b, _na_band_start(g) + i, _col(P_NK))))
        kvs.append((p, KV_CHUNK, GROUP_W,
                    lambda g, b, i=i: (b, _na_band_start(g) + i, _col(P_NV))))
    n_ctx = p_ctx.shape[1]
    kvs += [(p_ctx, n_ctx, GROUP_W, lambda g, b: (b, 0, _col(P_NK))),
            (p_ctx, n_ctx, GROUP_W, lambda g, b: (b, 0, _col(P_NV)))]
    chunks = tuple((i, 0) for i in range(NA_BAND)) + tuple(
        (NA_BAND, c * KV_CHUNK) for c in range(n_ctx // KV_CHUNK))
    pattern = lambda g, b: (jnp.where(g == 0, 0, jnp.where(g == NA_GROUPS - 1, 2, 1)), 0, 0)
    return _attention(
        grid=(NA_GROUPS, bsz // NA_NB), nb=NA_NB, q_rows=NA_TQ,
        q=(p, lambda g, b: (b, g, _col(P_NQ))),
        gate=(p, lambda g, b: (b, g, _col(P_GN))),
        kvs=kvs, bias=(bias, pattern),
        out_shape=(bsz, length, GROUP_W), out_map=lambda g, b: (b, g, 0),
        tq=NA_TQ, chunks=chunks, name="na_attention", fp8=fp8)


def _na_context(p_ctx, fp8):
    bsz, n_ctx, _ = p_ctx.shape
    kvs = [(p_ctx, n_ctx, GROUP_W, lambda b: (b, 0, _col(P_NK))),
           (p_ctx, n_ctx, GROUP_W, lambda b: (b, 0, _col(P_NV)))]
    return _attention(
        grid=(bsz // CTX_NB,), nb=CTX_NB, q_rows=n_ctx,
        q=(p_ctx, lambda b: (b, 0, _col(P_NQ))),
        gate=(p_ctx, lambda b: (b, 0, _col(P_GN))),
        kvs=kvs, bias=None,
        out_shape=(bsz, n_ctx, GROUP_W), out_map=lambda b: (b, 0, 0),
        tq=NA_TQ, chunks=tuple((0, c * KV_CHUNK) for c in range(n_ctx // KV_CHUNK)),
        name="na_context_attention", fp8=fp8)


def _na_bias_layout():
    rows = 2048 // GRID_W
    rows_per_tile, rows_per_chunk = NA_TQ // GRID_W, KV_CHUNK // GRID_W
    band_rows = NA_BAND * rows_per_chunk
    slot = np.full((len(NA_PATTERN_GROUPS), rows_per_tile, band_rows), 2 * NA_KH - 1)
    for pat, g in enumerate(NA_PATTERN_GROUPS):
        band0 = rows_per_chunk * min(max(g - 1, 0), NA_GROUPS - NA_BAND)
        for qr in range(rows_per_tile):
            r = rows_per_tile * g + qr
            r0 = min(max(r - NA_KH // 2, 0), rows - NA_KH)
            for kr in range(band_rows):
                rp = band0 + kr
                if r0 <= rp < r0 + NA_KH:
                    slot[pat, qr, kr] = rp - r + NA_KH - 1
    c = np.arange(GRID_W)
    c0 = np.clip(c - NA_KW // 2, 0, GRID_W - NA_KW)
    col_ok = (c[None, :] >= c0[:, None]) & (c[None, :] < c0[:, None] + NA_KW)
    return slot, col_ok


def _na_bias(rpb, slot, col_ok):
    heads, n_dr, _ = rpb.shape
    rpb = rpb * LOG2E
    w = jnp.concatenate([rpb[..., NA_KW - 1:],
                         jnp.full((heads, n_dr, LANES - (2 * NA_KW - 1)), NEG, F32),
                         rpb[..., :NA_KW - 1]], axis=-1)
    t = jnp.tile(w, (1, 1, GRID_W))[..., :GRID_W * (LANES - 1)]
    t = t.reshape(heads, n_dr, GRID_W, LANES - 1)[..., :GRID_W]
    t = jnp.where(col_ok[None, None], t, NEG)
    t = jnp.concatenate([t, jnp.full((heads, 1, GRID_W, GRID_W), NEG, F32)], axis=1)
    n_pat, n_qr, n_kr = slot.shape
    blocks = jnp.stack([t[:, int(s)] for s in slot.reshape(-1)], axis=1)
    blocks = blocks.reshape(heads, n_pat, n_qr, n_kr, GRID_W, GRID_W)
    blocks = jnp.transpose(blocks, (1, 0, 2, 4, 3, 5))
    return blocks.reshape(n_pat, heads * n_qr * GRID_W, n_kr * GRID_W)


POOL_HALO = 8
POOL_ROWS = 256


def _pool_kernel(z_ref, g_ref, cnt_ref, w_ref, s_ref, o_ref, zp_ref, *, length):
    zeros = jnp.zeros((POOL_HALO, GROUP_W), F32)
    zp_ref[0:POOL_HALO, :] = zeros
    zp_ref[POOL_HALO + length:2 * POOL_HALO + length, :] = zeros
    zp_ref[POOL_HALO:POOL_HALO + length, :] = z_ref[0].astype(F32)
    first = lax.broadcasted_iota(jnp.int32, (POOL_ROWS, LANES), 1) < HEAD_DIM

    def window(t0, lanes, offsets):
        acc = None
        for o in offsets:
            v = zp_ref[POOL_HALO + t0 + o:POOL_HALO + t0 + o + POOL_ROWS, lanes]
            acc = v if acc is None else acc + v
        return acc

    for t0 in range(0, length, POOL_ROWS):
        lo, hi = slice(0, LANES), slice(LANES, 2 * LANES)
        s2 = window(t0, lo, (-1, 0))
        s4 = s2 + window(t0, lo, (-2, 1))
        s8 = window(t0, hi, range(-4, 4))
        s16 = s8 + window(t0, hi, tuple(range(-8, -4)) + tuple(range(4, 8)))
        win = jnp.concatenate([jnp.where(first, s2, s4), jnp.where(first, s8, s16)], axis=-1)
        rows = slice(t0, t0 + POOL_ROWS)
        z = zp_ref[POOL_HALO + t0:POOL_HALO + t0 + POOL_ROWS, :]
        pooled = win / cnt_ref[rows, :] - z
        y = _dot(pooled.astype(BF16), w_ref[...]) * s_ref[...]
        o_ref[0, rows, :] = (y * g_ref[0, rows, :].astype(F32)).astype(BF16)


def _pool(p, cnt, w_bd, scale):
    bsz, length, _ = p.shape
    return pl.pallas_call(
        functools.partial(_pool_kernel, length=length),
        grid=(bsz,),
        in_specs=[
            pl.BlockSpec((1, length, GROUP_W), lambda b: (b, 0, _col(P_BZ))),
            pl.BlockSpec((1, length, GROUP_W), lambda b: (b, 0, _col(P_GB))),
            pl.BlockSpec((length, GROUP_W), lambda b: (0, 0)),
            pl.BlockSpec((GROUP_W, GROUP_W), lambda b: (0, 0)),
            pl.BlockSpec((1, GROUP_W), lambda b: (0, 0)),
        ],
        out_specs=pl.BlockSpec((1, length, GROUP_W), lambda b: (b, 0, 0)),
        out_shape=jax.ShapeDtypeStruct((bsz, length, GROUP_W), BF16),
        scratch_shapes=[pltpu.VMEM((length + 2 * POOL_HALO, GROUP_W), F32)],
        compiler_params=pltpu.CompilerParams(vmem_limit_bytes=VMEM_LIMIT),
        name="pool_mixer",
    )(p, p, cnt, w_bd, scale)


def _pool_counts(length):
    t = np.arange(length)
    cols = []
    for w in POOL_WINDOWS:
        lo = np.maximum(t - w // 2, 0)
        hi = np.minimum(t + w // 2 - 1, length - 1)
        cols.append(np.repeat((hi - lo + 1).astype(np.float32)[:, None], HEAD_DIM, axis=1))
    return np.concatenate(cols, axis=1)


X_RING = 3


def _out_kernel(x_hbm, gate_ref, a_ref, b_ref, n_ref, w_ref, o_ref, xbuf, sem, *,
                tm, per_seq, n_steps):
    s = pl.program_id(0)

    def x_copy(step, slot):
        r0 = (step % per_seq) * tm
        rows = pl.ds(r0 if isinstance(r0, int) else pl.multiple_of(r0, tm), tm)
        return pltpu.make_async_copy(x_hbm.at[step // per_seq, rows, :], xbuf.at[slot],
                                     sem.at[slot])

    @pl.when(s == 0)
    def _():
        for k in range(min(X_RING, n_steps)):
            x_copy(k, k).start()

    slot = s % X_RING
    y = (_dot(a_ref[0], w_ref[0:512, :]) + _dot(b_ref[0], w_ref[512:768, :])
         + _dot(n_ref[0], w_ref[768:1024, :]))
    x_copy(s, slot).wait()
    o_ref[0] = xbuf[slot] + gate_ref[0] * y

    @pl.when(s + X_RING < n_steps)
    def _():
        x_copy(s + X_RING, slot).start()


def _output(x, mods, mod_row, a, b, n, w_out, tm):
    bsz, length, _ = x.shape
    per_seq = length // tm
    n_steps = bsz * per_seq
    tok = lambda width: pl.BlockSpec((1, tm, width), lambda s: (s // per_seq, s % per_seq, 0))
    return pl.pallas_call(
        functools.partial(_out_kernel, tm=tm, per_seq=per_seq, n_steps=n_steps),
        grid=(n_steps,),
        in_specs=[
            pl.BlockSpec(memory_space=pl.ANY),
            pl.BlockSpec((1, 1, D_MODEL), lambda s: (mod_row(s // per_seq), 0, 2)),
            tok(2 * GROUP_W), tok(GROUP_W), tok(GROUP_W),
            pl.BlockSpec((D_MODEL, D_MODEL), lambda s: (0, 0)),
        ],
        out_specs=tok(D_MODEL),
        out_shape=jax.ShapeDtypeStruct(x.shape, F32),
        scratch_shapes=[pltpu.VMEM((X_RING, tm, D_MODEL), F32),
                        pltpu.SemaphoreType.DMA((X_RING,))],
        compiler_params=pltpu.CompilerParams(
            vmem_limit_bytes=VMEM_LIMIT, dimension_semantics=("arbitrary",)),
        name="output_projection",
    )(x, mods, a, b, n, w_out)


def _rope_tables(seq):
    t = jnp.arange(seq)
    half = HEAD_DIM // 4
    inv_freq = ROPE_THETA ** (-jnp.arange(half, dtype=jnp.float32) / half)
    cos, sa, sb = [], [], []
    zero = jnp.zeros((seq, half), F32)
    for pos in (t // GRID_W, t % GRID_W):
        ang = pos.astype(jnp.float32)[:, None] * inv_freq[None, :]
        cos += [jnp.cos(ang), jnp.cos(ang)]
        sa += [-jnp.sin(ang), zero]
        sb += [zero, jnp.sin(ang)]
    tile = lambda parts: jnp.tile(jnp.concatenate(parts, axis=-1), (1, LANES // HEAD_DIM))
    return tile(cos), tile(sa), tile(sb)


def _layout_constants():
    lane = np.arange(GROUP_W)
    seg = lane[:, None] // HEAD_DIM == lane[None, :] // HEAD_DIM
    src = np.arange(LANES)[:, None]
    col = np.arange(2 * GROUP_W)[None, :]
    rep_k = src == (col // GROUP_W) * HEAD_DIM + col % HEAD_DIM
    colv = np.arange(2 * LANES)[None, :]
    is_v = colv % LANES < HEAD_DIM
    rep_v = is_v & (src == (colv // LANES) * HEAD_DIM + colv % LANES)
    return (jnp.asarray(seg, BF16), jnp.asarray(rep_k, BF16), jnp.asarray(rep_v, BF16),
            jnp.asarray(~is_v, F32))


def kernel(x, c, ctx, c_ctx, norm_gain, w_mod, b_mod, w_in, att_q_gain, att_k_gain,
           pool_w, pool_scale, na_q_gain, na_k_gain, na_rpb, w_out):
    bsz, seq, _ = x.shape
    n_ctx = ctx.shape[1]
    assert seq == NA_GROUPS * NA_TQ and n_ctx % KV_CHUNK == 0 and bsz < MOD_ROWS
    assert bsz % CTX_NB == 0 and bsz % NA_NB == 0

    mods_all = _modulation(c, c_ctx, w_mod, b_mod)
    tables = _rope_tables(seq)
    tables_ctx = (jnp.ones((n_ctx, LANES), F32), jnp.zeros((n_ctx, LANES), F32),
                  jnp.zeros((n_ctx, LANES), F32))
    consts = _layout_constants()
    cnt, cnt_ctx = jnp.asarray(_pool_counts(seq)), jnp.asarray(_pool_counts(n_ctx))
    slot, col_ok = _na_bias_layout()
    w_in_b = w_in.astype(BF16)
    w_out_b = w_out.astype(BF16)
    wide = lambda g: jnp.tile(g, GROUP_W // HEAD_DIM)[None, :]
    lat_row = lambda b: b
    ctx_row = lambda b: bsz

    for l in range(DEPTH):
        last = l == DEPTH - 1
        mods = mods_all[l].reshape(MOD_ROWS, 1, 3 * D_MODEL)
        gains = (wide(att_q_gain[l]), wide(att_k_gain[l]), wide(na_q_gain[l]),
                 wide(na_k_gain[l]))
        ng = norm_gain[l][None, :]
        p = _projection(x, mods, lat_row, ng, w_in_b[l], tables, gains, consts, TOKEN_TILE)
        p_ctx = _projection(ctx, mods, ctx_row, ng, w_in_b[l], tables_ctx, gains, consts, 256,
                            kv_only=last)
        bias = _na_bias(na_rpb[l], slot, col_ok)
        w_bd = jax.scipy.linalg.block_diag(*[pool_w[l, g] for g in range(4)]).astype(BF16)
        ps = pool_scale[l][None, :]

        peak = lambda g: jnp.max(jnp.abs(g[l]))
        gains_ok = jnp.maximum(peak(att_q_gain) * peak(att_k_gain),
                               peak(na_q_gain) * peak(na_k_gain)) <= FP8_GAIN_LIMIT
        for g in (att_q_gain, att_k_gain, na_q_gain, na_k_gain):
            gains_ok &= peak(g) <= FP8_PEAK_GAIN
        value_peak = functools.reduce(jnp.maximum, [
            jnp.max(jnp.abs(arr[:, :, c0:c0 + GROUP_W]))
            for arr in (p, p_ctx) for c0 in (P_VE, P_NV)])
        use_fp8 = gains_ok & (value_peak <= F8_MAX)

        def mixers(fp8):
            outs = (_gqa_latent(p, p_ctx, fp8), _na_latent(p, p_ctx, bias, fp8))
            if not last:
                outs += (_gqa_context(p_ctx, fp8), _na_context(p_ctx, fp8))
            return outs

        mixed = lax.cond(use_fp8, lambda: mixers(True), lambda: mixers(False))
        a, n = mixed[:2]
        bo = _pool(p, cnt, w_bd, ps)
        x_new = _output(x, mods, lat_row, a, bo, n, w_out_b[l], OUT_TILE)

        if not last:
            a_c, n_c = mixed[2:]
            bo_c = _pool(p_ctx, cnt_ctx, w_bd, ps)
            ctx = _output(ctx, mods, ctx_row, a_c, bo_c, n_c, w_out_b[l], 256)
        x = x_new
    return x
```

```python
import functools
import math

import numpy as np
import jax
import jax.numpy as jnp
from jax import lax
from jax.experimental import pallas as pl
from jax.experimental.pallas import tpu as pltpu

F32 = jnp.float32
BF16 = jnp.bfloat16
F8 = jnp.float8_e4m3fn
P_SHIFT = 8.0
F8_MAX = 448.0
FP8_GAIN_LIMIT = 2.0
FP8_PEAK_GAIN = 0.98 * F8_MAX / (8.0 * 2.0 ** 0.5)

D_MODEL = 1024
DEPTH = 4
GRID_W = 64
HEAD_DIM = 64
ROPE_THETA = 10000.0
EPS = 1e-6
ATTN_SCALE = HEAD_DIM ** -0.5
LOG2E = 1.4426950408889634
Q_SCALE = ATTN_SCALE * LOG2E
POOL_WINDOWS = (2, 4, 8, 16)
NA_KH = 8
NA_KW = 16
IN_WIDTH = 2816

LANES = 128
GROUP_W = 256
HEADS_PER_GROUP = GROUP_W // HEAD_DIM
KV_CHUNK = 256
TILES_PER_ITER = 16

W_AQ, W_AK, W_AV, W_AG, W_BZ, W_BG, W_NQ, W_NK, W_NV, W_NG = (
    0, 512, 640, 768, 1280, 1536, 1792, 2048, 2304, 2560)
P_QA, P_KE, P_VE, P_GA, P_BZ, P_GB, P_NQ, P_NK, P_NV, P_GN = (
    0, 512, 1024, 1280, 1792, 2048, 2304, 2560, 2816, 3072)
P_WIDTH = 3328
MOD_ROWS = 24
TOKEN_TILE = 512
OUT_TILE = 1024
NEG = -1e30
VMEM_LIMIT = 48 * 1024 * 1024


def _silu(v):
    return v / (1.0 + jnp.exp(-v))


def _split_bf16(v):
    hi = v.astype(BF16)
    lo = (v - hi.astype(F32)).astype(BF16)
    return hi, lo


def _dot(a, b):
    return jnp.dot(a, b, preferred_element_type=F32)


def _mod_kernel(c_ref, w_ref, b_ref, o_ref):
    a_hi, a_lo = _split_bf16(_silu(c_ref[...]))
    w_hi, w_lo = _split_bf16(w_ref[0])
    o_ref[0] = _dot(a_hi, w_hi) + _dot(a_lo, w_hi) + _dot(a_hi, w_lo) + b_ref[0]


def _modulation(c, c_ctx, w_mod, b_mod):
    bsz = c.shape[0]
    cs = jnp.concatenate(
        [c, c_ctx[None], jnp.zeros((MOD_ROWS - bsz - 1, D_MODEL), F32)], axis=0)
    return pl.pallas_call(
        _mod_kernel,
        grid=(DEPTH, 3),
        in_specs=[
            pl.BlockSpec((MOD_ROWS, D_MODEL), lambda l, j: (0, 0)),
            pl.BlockSpec((1, D_MODEL, D_MODEL), lambda l, j: (l, 0, j)),
            pl.BlockSpec((1, 1, D_MODEL), lambda l, j: (l, 0, j)),
        ],
        out_specs=pl.BlockSpec((1, MOD_ROWS, D_MODEL), lambda l, j: (l, 0, j)),
        out_shape=jax.ShapeDtypeStruct((DEPTH, MOD_ROWS, 3 * D_MODEL), F32),
        compiler_params=pltpu.CompilerParams(vmem_limit_bytes=VMEM_LIMIT),
        name="modulation",
    )(cs, w_mod, b_mod.reshape(DEPTH, 1, 3 * D_MODEL))


def _proj_kernel(x_ref, shift_ref, scale_ref, ng_ref, w_ref, cos_ref, sa_ref, sb_ref,
                 aqg_ref, akg_ref, nqg_ref, nkg_ref, seg_ref, repk_ref, repv_ref, ones_ref,
                 o_ref, hb0_ref, hb1_ref, y0_ref, y1_ref, *, kv_only):
    step = pl.program_id(0)

    @pl.when(step == 0)
    def _():
        hb1_ref[...] = jnp.zeros_like(hb1_ref)
        y0_ref[...] = jnp.zeros_like(y0_ref)
        y1_ref[...] = jnp.zeros_like(y1_ref)

    def pre_norm(hb_ref):
        x = x_ref[0]
        ms = jnp.mean(x * x, axis=-1, keepdims=True)
        h = x * lax.rsqrt(ms + EPS) * ng_ref[...]
        h = h * (1.0 + scale_ref[0]) + shift_ref[0]
        hb_ref[...] = h.astype(BF16)

    def stage(hb_w, hb_r, y_w, y_r):
        pre_norm(hb_w)
        epilogue = _proj_epilogue(y_r, cos_ref, sa_ref, sb_ref, aqg_ref, akg_ref, nqg_ref,
                                  nkg_ref, seg_ref, repk_ref, repv_ref, ones_ref, o_ref, kv_only)
        for c0 in range(0, IN_WIDTH, GROUP_W):
            if not kv_only or c0 in (W_AK, W_NK, W_NV):
                y_w[:, c0:c0 + GROUP_W] = _dot(hb_r[...], w_ref[:, c0:c0 + GROUP_W])
            next(epilogue)

    @pl.when(step % 2 == 0)
    def _():
        stage(hb0_ref, hb1_ref, y1_ref, y0_ref)

    @pl.when(step % 2 == 1)
    def _():
        stage(hb1_ref, hb0_ref, y0_ref, y1_ref)


def _proj_epilogue(y_ref, cos_ref, sa_ref, sb_ref, aqg_ref, akg_ref, nqg_ref, nkg_ref,
                   seg_ref, repk_ref, repv_ref, ones_ref, o_ref, kv_only):
    def proj(c0, width):
        return y_ref[:, c0:c0 + width]

    def head_norm(y, g):
        ss = _dot((y * y).astype(BF16), seg_ref[...])
        return y * lax.rsqrt(ss * (1.0 / HEAD_DIM) + EPS) * g

    def rope(y):
        width = y.shape[-1]
        wide = lambda t: jnp.tile(t[...], (1, width // LANES))
        return (y * wide(cos_ref) + pltpu.roll(y, width - 16, 1) * wide(sa_ref)
                + pltpu.roll(y, 16, 1) * wide(sb_ref))

    def put(c0, v):
        o_ref[0, :, c0:c0 + v.shape[-1]] = v.astype(BF16)

    def unless_kv_only(c0, make):
        if kv_only:
            o_ref[0, :, c0:c0 + GROUP_W] = jnp.zeros((y_ref.shape[0], GROUP_W), BF16)
        else:
            put(c0, make())

    for j in range(2):
        unless_kv_only(P_QA + j * GROUP_W, lambda: rope(head_norm(
            proj(W_AQ + j * GROUP_W, GROUP_W), aqg_ref[...])) * Q_SCALE)
        yield
    kv = proj(W_AK, GROUP_W)
    kb = rope(head_norm(kv, akg_ref[...]))[:, :LANES].astype(BF16)
    put(P_KE, _dot(kb, repk_ref[...]))
    put(P_VE, _dot(kv[:, LANES:].astype(BF16), repv_ref[...]) + ones_ref[...])
    yield
    for j in range(2):
        unless_kv_only(P_GA + j * GROUP_W,
                       lambda: _silu(proj(W_AG + j * GROUP_W, GROUP_W)))
        yield
    unless_kv_only(P_BZ, lambda: proj(W_BZ, GROUP_W))
    yield
    unless_kv_only(P_GB, lambda: _silu(proj(W_BG, GROUP_W)))
    yield
    unless_kv_only(P_NQ, lambda: head_norm(proj(W_NQ, GROUP_W), nqg_ref[...]) * Q_SCALE)
    yield
    put(P_NK, head_norm(proj(W_NK, GROUP_W), nkg_ref[...]))
    yield
    put(P_NV, proj(W_NV, GROUP_W))
    yield
    unless_kv_only(P_GN, lambda: _silu(proj(W_NG, GROUP_W)))
    yield


def _projection(x, mods, mod_row, norm_gain, w_in, tables, gains, consts, tm, kv_only=False):
    bsz, length, _ = x.shape
    cos, sa, sb = tables
    per_seq = length // tm
    n_tiles = bsz * per_seq
    norm_tile = lambda i: jnp.minimum(i, n_tiles - 1)
    out_tile = lambda i: jnp.maximum(i - 2, 0)
    vec = lambda k: pl.BlockSpec(
        (1, 1, D_MODEL), lambda i: (mod_row(norm_tile(i) // per_seq), 0, k))
    const2 = lambda a: pl.BlockSpec(a.shape, lambda i: (0, 0))
    tab = pl.BlockSpec((tm, LANES), lambda i: (out_tile(i) % per_seq, 0))
    hidden = pltpu.VMEM((tm, D_MODEL), BF16)
    projected = pltpu.VMEM((tm, IN_WIDTH), F32)
    return pl.pallas_call(
        functools.partial(_proj_kernel, kv_only=kv_only),
        grid=(n_tiles + 2,),
        in_specs=[
            pl.BlockSpec((1, tm, D_MODEL),
                         lambda i: (norm_tile(i) // per_seq, norm_tile(i) % per_seq, 0)),
            vec(0), vec(1), const2(norm_gain),
            pl.BlockSpec(w_in.shape, lambda i: (0, 0), pipeline_mode=pl.Buffered(1)),
            tab, tab, tab,
            *[const2(g) for g in gains], *[const2(a) for a in consts],
        ],
        out_specs=pl.BlockSpec((1, tm, P_WIDTH),
                               lambda i: (out_tile(i) // per_seq, out_tile(i) % per_seq, 0)),
        out_shape=jax.ShapeDtypeStruct((bsz, length, P_WIDTH), BF16),
        scratch_shapes=[hidden, hidden, projected, projected],
        compiler_params=pltpu.CompilerParams(
            vmem_limit_bytes=VMEM_LIMIT, dimension_semantics=("arbitrary",)),
        name="projection",
    )(x, mods, mods, norm_gain, w_in, cos, sa, sb, *gains, *consts)


def _attn_kernel(*refs, tq, tiles_per_row, n_tiles, tiles_per_iter, n_kv, chunks, n_bias,
                 sums_in_v, fp8):
    q_ref, g_ref = refs[0], refs[1]
    kv = refs[2:2 + 2 * n_kv]
    pos = 2 + 2 * n_kv
    bias_ref = refs[pos] if n_bias else None
    pos += 1 if n_bias else 0
    o_ref, s_a, s_b, m_a, m_b, mc_ref, lc_ref = refs[pos:pos + 7]
    mm_dtype = BF16
    if fp8:
        mm_dtype = F8
        kv8 = refs[pos + 7:pos + 7 + 2 * n_kv]
        for src, dst in zip(kv, kv8):
            dst[...] = src[...].astype(F8)
        kv = kv8
    rows = HEADS_PER_GROUP * tq
    n_chunks = len(chunks)
    lane_head = jnp.right_shift(lax.broadcasted_iota(jnp.int32, (tq, GROUP_W), 1), 6)
    low_half = lax.broadcasted_iota(jnp.int32, (tq, LANES), 1) < HEAD_DIM
    shift = tiles_per_row.bit_length() - 1

    def tile_pos(t):
        if tiles_per_row == 1:
            return t, 0
        r0 = (t & (tiles_per_row - 1)) * tq
        return t >> shift, (r0 if isinstance(r0, int) else pl.multiple_of(r0, tq))

    def stage(t1, s1, m1, t0, s0, m0):
        if t1 is not None:
            b1, r1 = tile_pos(t1)
            q = q_ref[b1, pl.ds(r1, tq), :].astype(F32)
            qs = jnp.concatenate([jnp.where(lane_head == h, q, 0.0).astype(mm_dtype)
                                  for h in range(HEADS_PER_GROUP)], axis=0)
        if t0 is not None:
            b0, r0 = tile_pos(t0)
            m_prev = m0[...]
            m_prev = jnp.concatenate([m_prev, m_prev], axis=-1)
            acc = None
        for ci, (ai, off) in enumerate(chunks):
            if t1 is not None:
                k = kv[2 * ai][b1, off:off + KV_CHUNK, :]
                s = lax.dot_general(qs, k, (((1,), (1,)), ((), ())),
                                    preferred_element_type=F32)
                if ci < n_bias:
                    s = s + bias_ref[0, :, ci * KV_CHUNK:(ci + 1) * KV_CHUNK]
                s1[ci] = s
                mc_ref[ci] = jnp.maximum(s[:, :LANES], s[:, LANES:])
            if t0 is not None:
                z = s0[ci] - m_prev
                p = jnp.exp2(z.astype(BF16) if fp8 else z)
                if not sums_in_v:
                    lc_ref[ci] = (p[:, :LANES] + p[:, LANES:]).astype(F32)
                d = _dot(p.astype(mm_dtype), kv[2 * ai + 1][b0, off:off + KV_CHUNK, :])
                acc = d if acc is None else acc + d
        if t1 is not None:
            m = mc_ref[0]
            for ci in range(1, n_chunks):
                m = jnp.maximum(m, mc_ref[ci])
            m = jnp.max(m, axis=-1, keepdims=True) - (P_SHIFT if fp8 else 0.0)
            m1[...] = jnp.broadcast_to(m, (rows, LANES))
        if t0 is not None:
            gate = g_ref[b0, pl.ds(r0, tq), :].astype(F32)
            head = lambda a, h: a[h * tq:(h + 1) * tq, :]
            if sums_in_v:
                o = acc / pltpu.roll(acc, HEAD_DIM, 1)
                halves = [jnp.where(low_half, head(o, h), pltpu.roll(head(o, h + 1), HEAD_DIM, 1))
                          for h in (0, 2)]
                out = jnp.concatenate(halves, axis=-1)
            else:
                l = lc_ref[0]
                for ci in range(1, n_chunks):
                    l = l + lc_ref[ci]
                o = acc / jnp.sum(l, axis=-1, keepdims=True)
                out = jnp.zeros((tq, GROUP_W), F32)
                for h in range(HEADS_PER_GROUP):
                    out = out + jnp.where(lane_head == h, head(o, h), 0.0)
            o_ref[b0, pl.ds(r0, tq), :] = (out * gate).astype(BF16)

    stage(0, s_a, m_a, None, None, None)

    if tiles_per_iter == n_tiles:
        for t in range(0, n_tiles, 2):
            stage(t + 1, s_b, m_b, t, s_a, m_a)
            stage(t + 2 if t + 2 < n_tiles else None, s_a, m_a, t + 1, s_b, m_b)
        return

    def body(j, carry):
        t = tiles_per_iter * j
        for u in range(0, tiles_per_iter, 2):
            stage(t + u + 1, s_b, m_b, t + u, s_a, m_a)
            stage(jnp.minimum(t + u + 2, n_tiles - 1), s_a, m_a, t + u + 1, s_b, m_b)
        return carry

    lax.fori_loop(0, n_tiles // tiles_per_iter, body, 0)


def _attention(grid, nb, q_rows, q, gate, kvs, bias, out_shape, out_map, tq, chunks, name,
               fp8=False):
    n_kv = len(kvs) // 2
    n_bias = 0
    operands = [q[0], gate[0]]
    specs = [pl.BlockSpec((nb, q_rows, GROUP_W), q[1]),
             pl.BlockSpec((nb, q_rows, GROUP_W), gate[1])]
    for arr, kv_rows, kv_lanes, imap in kvs:
        operands.append(arr)
        specs.append(pl.BlockSpec((nb, kv_rows, kv_lanes), imap))
    sums_in_v = kvs[1][2] == LANES
    if bias is not None:
        arr, imap = bias
        n_bias = arr.shape[-1] // KV_CHUNK
        operands.append(arr)
        specs.append(pl.BlockSpec((1,) + arr.shape[1:], imap))
    rows = HEADS_PER_GROUP * tq
    tiles_per_row = q_rows // tq
    n_tiles = nb * tiles_per_row
    assert n_tiles % 2 == 0 and tiles_per_row & (tiles_per_row - 1) == 0
    score_buf = pltpu.VMEM((len(chunks), rows, KV_CHUNK), F32)
    max_buf = pltpu.VMEM((rows, LANES), F32)
    part_buf = pltpu.VMEM((len(chunks), rows, LANES), F32)
    sum_buf = pltpu.VMEM((1, 8, LANES) if sums_in_v else (len(chunks), rows, LANES), F32)
    kv8 = [pltpu.VMEM((nb, kv_rows, kv_lanes), F8) for _, kv_rows, kv_lanes, _ in kvs] if fp8 else []
    return pl.pallas_call(
        functools.partial(_attn_kernel, tq=tq, tiles_per_row=tiles_per_row, n_tiles=n_tiles,
                          tiles_per_iter=math.gcd(n_tiles, TILES_PER_ITER), n_kv=n_kv,
                          chunks=chunks, n_bias=n_bias, sums_in_v=sums_in_v, fp8=fp8),
        grid=grid,
        in_specs=specs,
        out_specs=pl.BlockSpec((nb, q_rows, GROUP_W), out_map),
        out_shape=jax.ShapeDtypeStruct(out_shape, BF16),
        scratch_shapes=[score_buf, score_buf, max_buf, max_buf, part_buf, sum_buf, *kv8],
        compiler_params=pltpu.CompilerParams(vmem_limit_bytes=VMEM_LIMIT),
        name=name,
    )(*operands)


def _col(off):
    return off // GROUP_W


GQA_TQ = 128
CTX_NB = 8


def _gqa_kv(p):
    rows = p.shape[1]
    return [(p, rows, GROUP_W, lambda b, j: (b, 0, _col(P_KE) + j)),
            (p, rows, LANES, lambda b, j: (b, 0, P_VE // LANES + j))]


def _gqa_latent(p, p_ctx, fp8):
    bsz, length, _ = p.shape
    n_ctx = p_ctx.shape[1]
    kvs = _gqa_kv(p) + _gqa_kv(p_ctx)
    chunks = tuple((0, c * KV_CHUNK) for c in range(length // KV_CHUNK)) + tuple(
        (1, c * KV_CHUNK) for c in range(n_ctx // KV_CHUNK))
    return _attention(
        grid=(bsz, 2), nb=1, q_rows=length,
        q=(p, lambda b, j: (b, 0, _col(P_QA) + j)),
        gate=(p, lambda b, j: (b, 0, _col(P_GA) + j)),
        kvs=kvs, bias=None,
        out_shape=(bsz, length, 2 * GROUP_W), out_map=lambda b, j: (b, 0, j),
        tq=GQA_TQ, chunks=chunks, name="gqa_attention", fp8=fp8)


def _gqa_context(p_ctx, fp8):
    bsz, n_ctx, _ = p_ctx.shape
    return _attention(
        grid=(bsz // CTX_NB, 2), nb=CTX_NB, q_rows=n_ctx,
        q=(p_ctx, lambda b, j: (b, 0, _col(P_QA) + j)),
        gate=(p_ctx, lambda b, j: (b, 0, _col(P_GA) + j)),
        kvs=_gqa_kv(p_ctx), bias=None,
        out_shape=(bsz, n_ctx, 2 * GROUP_W), out_map=lambda b, j: (b, 0, j),
        tq=GQA_TQ, chunks=tuple((0, c * KV_CHUNK) for c in range(n_ctx // KV_CHUNK)),
        name="gqa_context_attention", fp8=fp8)


NA_TQ = 256
NA_BAND = 3
NA_GROUPS = 2048 // NA_TQ
NA_NB = 4
NA_PATTERN_GROUPS = (0, 1, NA_GROUPS - 1)


def _na_band_start(g):
    return jnp.clip(g - 1, 0, NA_GROUPS - NA_BAND)


def _na_latent(p, p_ctx, bias, fp8):
    bsz, length, _ = p.shape
    kvs = []
    for i in range(NA_BAND):
        kvs.append((p, KV_CHUNK, GROUP_W,
                    lambda g, b, i=i: (b, _na_band_start(g) + i, _col(P_NK))))
        kvs.append((p, KV_CHUNK, GROUP_W,
                    lambda g, b, i=i: (b, _na_band_start(g) + i, _col(P_NV))))
    n_ctx = p_ctx.shape[1]
    kvs += [(p_ctx, n_ctx, GROUP_W, lambda g, b: (b, 0, _col(P_NK))),
            (p_ctx, n_ctx, GROUP_W, lambda g, b: (b, 0, _col(P_NV)))]
    chunks = tuple((i, 0) for i in range(NA_BAND)) + tuple(
        (NA_BAND, c * KV_CHUNK) for c in range(n_ctx // KV_CHUNK))
    pattern = lambda g, b: (jnp.where(g == 0, 0, jnp.where(g == NA_GROUPS - 1, 2, 1)), 0, 0)
    return _attention(
        grid=(NA_GROUPS, bsz // NA_NB), nb=NA_NB, q_rows=NA_TQ,
        q=(p, lambda g, b: (b, g, _col(P_NQ))),
        gate=(p, lambda g, b: (b, g, _col(P_GN))),
        kvs=kvs, bias=(bias, pattern),
        out_shape=(bsz, length, GROUP_W), out_map=lambda g, b: (b, g, 0),
        tq=NA_TQ, chunks=chunks, name="na_attention", fp8=fp8)


def _na_context(p_ctx, fp8):
    bsz, n_ctx, _ = p_ctx.shape
    kvs = [(p_ctx, n_ctx, GROUP_W, lambda b: (b, 0, _col(P_NK))),
           (p_ctx, n_ctx, GROUP_W, lambda b: (b, 0, _col(P_NV)))]
    return _attention(
        grid=(bsz // CTX_NB,), nb=CTX_NB, q_rows=n_ctx,
        q=(p_ctx, lambda b: (b, 0, _col(P_NQ))),
        gate=(p_ctx, lambda b: (b, 0, _col(P_GN))),
        kvs=kvs, bias=None,
        out_shape=(bsz, n_ctx, GROUP_W), out_map=lambda b: (b, 0, 0),
        tq=NA_TQ, chunks=tuple((0, c * KV_CHUNK) for c in range(n_ctx // KV_CHUNK)),
        name="na_context_attention", fp8=fp8)


def _na_bias_layout():
    rows = 2048 // GRID_W
    rows_per_tile, rows_per_chunk = NA_TQ // GRID_W, KV_CHUNK // GRID_W
    band_rows = NA_BAND * rows_per_chunk
    slot = np.full((len(NA_PATTERN_GROUPS), rows_per_tile, band_rows), 2 * NA_KH - 1)
    for pat, g in enumerate(NA_PATTERN_GROUPS):
        band0 = rows_per_chunk * min(max(g - 1, 0), NA_GROUPS - NA_BAND)
        for qr in range(rows_per_tile):
            r = rows_per_tile * g + qr
            r0 = min(max(r - NA_KH // 2, 0), rows - NA_KH)
            for kr in range(band_rows):
                rp = band0 + kr
                if r0 <= rp < r0 + NA_KH:
                    slot[pat, qr, kr] = rp - r + NA_KH - 1
    c = np.arange(GRID_W)
    c0 = np.clip(c - NA_KW // 2, 0, GRID_W - NA_KW)
    col_ok = (c[None, :] >= c0[:, None]) & (c[None, :] < c0[:, None] + NA_KW)
    return slot, col_ok


def _na_bias(rpb, slot, col_ok):
    heads, n_dr, _ = rpb.shape
    rpb = rpb * LOG2E
    w = jnp.concatenate([rpb[..., NA_KW - 1:],
                         jnp.full((heads, n_dr, LANES - (2 * NA_KW - 1)), NEG, F32),
                         rpb[..., :NA_KW - 1]], axis=-1)
    t = jnp.tile(w, (1, 1, GRID_W))[..., :GRID_W * (LANES - 1)]
    t = t.reshape(heads, n_dr, GRID_W, LANES - 1)[..., :GRID_W]
    t = jnp.where(col_ok[None, None], t, NEG)
    t = jnp.concatenate([t, jnp.full((heads, 1, GRID_W, GRID_W), NEG, F32)], axis=1)
    n_pat, n_qr, n_kr = slot.shape
    blocks = jnp.stack([t[:, int(s)] for s in slot.reshape(-1)], axis=1)
    blocks = blocks.reshape(heads, n_pat, n_qr, n_kr, GRID_W, GRID_W)
    blocks = jnp.transpose(blocks, (1, 0, 2, 4, 3, 5))
    return blocks.reshape(n_pat, heads * n_qr * GRID_W, n_kr * GRID_W)


POOL_HALO = 8
POOL_ROWS = 256


def _pool_kernel(z_ref, g_ref, cnt_ref, w_ref, s_ref, o_ref, zp_ref, *, length):
    zeros = jnp.zeros((POOL_HALO, GROUP_W), F32)
    zp_ref[0:POOL_HALO, :] = zeros
    zp_ref[POOL_HALO + length:2 * POOL_HALO + length, :] = zeros
    zp_ref[POOL_HALO:POOL_HALO + length, :] = z_ref[0].astype(F32)
    first = lax.broadcasted_iota(jnp.int32, (POOL_ROWS, LANES), 1) < HEAD_DIM

    def window(t0, lanes, offsets):
        acc = None
        for o in offsets:
            v = zp_ref[POOL_HALO + t0 + o:POOL_HALO + t0 + o + POOL_ROWS, lanes]
            acc = v if acc is None else acc + v
        return acc

    for t0 in range(0, length, POOL_ROWS):
        lo, hi = slice(0, LANES), slice(LANES, 2 * LANES)
        s2 = window(t0, lo, (-1, 0))
        s4 = s2 + window(t0, lo, (-2, 1))
        s8 = window(t0, hi, range(-4, 4))
        s16 = s8 + window(t0, hi, tuple(range(-8, -4)) + tuple(range(4, 8)))
        win = jnp.concatenate([jnp.where(first, s2, s4), jnp.where(first, s8, s16)], axis=-1)
        rows = slice(t0, t0 + POOL_ROWS)
        z = zp_ref[POOL_HALO + t0:POOL_HALO + t0 + POOL_ROWS, :]
        pooled = win / cnt_ref[rows, :] - z
        y = _dot(pooled.astype(BF16), w_ref[...]) * s_ref[...]
        o_ref[0, rows, :] = (y * g_ref[0, rows, :].astype(F32)).astype(BF16)


def _pool(p, cnt, w_bd, scale):
    bsz, length, _ = p.shape
    return pl.pallas_call(
        functools.partial(_pool_kernel, length=length),
        grid=(bsz,),
        in_specs=[
            pl.BlockSpec((1, length, GROUP_W), lambda b: (b, 0, _col(P_BZ))),
            pl.BlockSpec((1, length, GROUP_W), lambda b: (b, 0, _col(P_GB))),
            pl.BlockSpec((length, GROUP_W), lambda b: (0, 0)),
            pl.BlockSpec((GROUP_W, GROUP_W), lambda b: (0, 0)),
            pl.BlockSpec((1, GROUP_W), lambda b: (0, 0)),
        ],
        out_specs=pl.BlockSpec((1, length, GROUP_W), lambda b: (b, 0, 0)),
        out_shape=jax.ShapeDtypeStruct((bsz, length, GROUP_W), BF16),
        scratch_shapes=[pltpu.VMEM((length + 2 * POOL_HALO, GROUP_W), F32)],
        compiler_params=pltpu.CompilerParams(vmem_limit_bytes=VMEM_LIMIT),
        name="pool_mixer",
    )(p, p, cnt, w_bd, scale)


def _pool_counts(length):
    t = np.arange(length)
    cols = []
    for w in POOL_WINDOWS:
        lo = np.maximum(t - w // 2, 0)
        hi = np.minimum(t + w // 2 - 1, length - 1)
        cols.append(np.repeat((hi - lo + 1).astype(np.float32)[:, None], HEAD_DIM, axis=1))
    return np.concatenate(cols, axis=1)


def _out_kernel(x_ref, gate_ref, a_ref, b_ref, n_ref, w_ref, o_ref):
    y = (_dot(a_ref[0], w_ref[0:512, :]) + _dot(b_ref[0], w_ref[512:768, :])
         + _dot(n_ref[0], w_ref[768:1024, :]))
    o_ref[0] = x_ref[0] + gate_ref[0] * y


def _output(x, mods, mod_row, a, b, n, w_out, tm):
    bsz, length, _ = x.shape
    tok = lambda width: pl.BlockSpec((1, tm, width), lambda bb, i: (bb, i, 0))
    return pl.pallas_call(
        _out_kernel,
        grid=(bsz, length // tm),
        in_specs=[
            tok(D_MODEL),
            pl.BlockSpec((1, 1, D_MODEL), lambda bb, i: (mod_row(bb), 0, 2)),
            tok(2 * GROUP_W), tok(GROUP_W), tok(GROUP_W),
            pl.BlockSpec((D_MODEL, D_MODEL), lambda bb, i: (0, 0),
                         pipeline_mode=pl.Buffered(1)),
        ],
        out_specs=tok(D_MODEL),
        out_shape=jax.ShapeDtypeStruct(x.shape, F32),
        compiler_params=pltpu.CompilerParams(vmem_limit_bytes=VMEM_LIMIT),
        name="output_projection",
    )(x, mods, a, b, n, w_out)


def _rope_tables(seq):
    t = jnp.arange(seq)
    half = HEAD_DIM // 4
    inv_freq = ROPE_THETA ** (-jnp.arange(half, dtype=jnp.float32) / half)
    cos, sa, sb = [], [], []
    zero = jnp.zeros((seq, half), F32)
    for pos in (t // GRID_W, t % GRID_W):
        ang = pos.astype(jnp.float32)[:, None] * inv_freq[None, :]
        cos += [jnp.cos(ang), jnp.cos(ang)]
        sa += [-jnp.sin(ang), zero]
        sb += [zero, jnp.sin(ang)]
    tile = lambda parts: jnp.tile(jnp.concatenate(parts, axis=-1), (1, LANES // HEAD_DIM))
    return tile(cos), tile(sa), tile(sb)


def _layout_constants():
    lane = np.arange(GROUP_W)
    seg = lane[:, None] // HEAD_DIM == lane[None, :] // HEAD_DIM
    src = np.arange(LANES)[:, None]
    col = np.arange(2 * GROUP_W)[None, :]
    rep_k = src == (col // GROUP_W) * HEAD_DIM + col % HEAD_DIM
    colv = np.arange(2 * LANES)[None, :]
    is_v = colv % LANES < HEAD_DIM
    rep_v = is_v & (src == (colv // LANES) * HEAD_DIM + colv % LANES)
    return (jnp.asarray(seg, BF16), jnp.asarray(rep_k, BF16), jnp.asarray(rep_v, BF16),
            jnp.asarray(~is_v, F32))


def kernel(x, c, ctx, c_ctx, norm_gain, w_mod, b_mod, w_in, att_q_gain, att_k_gain,
           pool_w, pool_scale, na_q_gain, na_k_gain, na_rpb, w_out):
    bsz, seq, _ = x.shape
    n_ctx = ctx.shape[1]
    assert seq == NA_GROUPS * NA_TQ and n_ctx % KV_CHUNK == 0 and bsz < MOD_ROWS
    assert bsz % CTX_NB == 0 and bsz % NA_NB == 0

    mods_all = _modulation(c, c_ctx, w_mod, b_mod)
    tables = _rope_tables(seq)
    tables_ctx = (jnp.ones((n_ctx, LANES), F32), jnp.zeros((n_ctx, LANES), F32),
                  jnp.zeros((n_ctx, LANES), F32))
    consts = _layout_constants()
    cnt, cnt_ctx = jnp.asarray(_pool_counts(seq)), jnp.asarray(_pool_counts(n_ctx))
    slot, col_ok = _na_bias_layout()
    w_in_b = w_in.astype(BF16)
    w_out_b = w_out.astype(BF16)
    wide = lambda g: jnp.tile(g, GROUP_W // HEAD_DIM)[None, :]
    lat_row = lambda b: b
    ctx_row = lambda b: bsz

    for l in range(DEPTH):
        last = l == DEPTH - 1
        mods = mods_all[l].reshape(MOD_ROWS, 1, 3 * D_MODEL)
        gains = (wide(att_q_gain[l]), wide(att_k_gain[l]), wide(na_q_gain[l]),
                 wide(na_k_gain[l]))
        ng = norm_gain[l][None, :]
        p = _projection(x, mods, lat_row, ng, w_in_b[l], tables, gains, consts, TOKEN_TILE)
        p_ctx = _projection(ctx, mods, ctx_row, ng, w_in_b[l], tables_ctx, gains, consts, 256,
                            kv_only=last)
        bias = _na_bias(na_rpb[l], slot, col_ok)
        w_bd = jax.scipy.linalg.block_diag(*[pool_w[l, g] for g in range(4)]).astype(BF16)
        ps = pool_scale[l][None, :]

        peak = lambda g: jnp.max(jnp.abs(g[l]))
        gains_ok = jnp.maximum(peak(att_q_gain) * peak(att_k_gain),
                               peak(na_q_gain) * peak(na_k_gain)) <= FP8_GAIN_LIMIT
        for g in (att_q_gain, att_k_gain, na_q_gain, na_k_gain):
            gains_ok &= peak(g) <= FP8_PEAK_GAIN
        value_peak = functools.reduce(jnp.maximum, [
            jnp.max(jnp.abs(arr[:, :, c0:c0 + GROUP_W]))
            for arr in (p, p_ctx) for c0 in (P_VE, P_NV)])
        use_fp8 = gains_ok & (value_peak <= F8_MAX)

        def mixers(fp8):
            outs = (_gqa_latent(p, p_ctx, fp8), _na_latent(p, p_ctx, bias, fp8))
            if not last:
                outs += (_gqa_context(p_ctx, fp8), _na_context(p_ctx, fp8))
            return outs

        mixed = lax.cond(use_fp8, lambda: mixers(True), lambda: mixers(False))
        a, n = mixed[:2]
        bo = _pool(p, cnt, w_bd, ps)
        x_new = _output(x, mods, lat_row, a, bo, n, w_out_b[l], OUT_TILE)

        if not last:
            a_c, n_c = mixed[2:]
            bo_c = _pool(p_ctx, cnt_ctx, w_bd, ps)
            ctx = _output(ctx, mods, ctx_row, a_c, bo_c, n_c, w_out_b[l], 256)
        x = x_new
    return x
```
